```python
import math
import jax, jax.numpy as jnp
from jax import lax
import numpy as np

D_MODEL = 1024
BATCH = 8
SEQ = 2048
DEPTH = 1
DEC_BATCH = 2
DEC_SEQ = 16384
PAST_LEN = 128

CONV_WIDTH = D_MODEL // 2
CONV_GROUPS = 8
CONV_TAPS = 3
N_HEADS = 8
QK_NOPE_DIM = 64
QK_ROPE_DIM = 32
V_HEAD_DIM = 64
ATTN_WIDTH = N_HEADS * V_HEAD_DIM
Q_LORA_RANK = 384
KV_LORA_RANK = 256
MIX_WIDTH = CONV_WIDTH + ATTN_WIDTH
ROPE_THETA = 10000.0
Q_BLOCK = 128
NORM_EPS = 1e-6
SPLIT_SIZES = (CONV_WIDTH, CONV_WIDTH, CONV_WIDTH, CONV_WIDTH,
               Q_LORA_RANK, KV_LORA_RANK, QK_ROPE_DIM, ATTN_WIDTH)
IN_WIDTH = 4 * CONV_WIDTH + Q_LORA_RANK + KV_LORA_RANK + QK_ROPE_DIM + ATTN_WIDTH

kernel_name = "hymba_shortconv_mla_sandwich_encoder"


def _rmsnorm(x, g):
    xf = x.astype(jnp.float32)
    y = xf * lax.rsqrt(jnp.mean(xf * xf, axis=-1, keepdims=True) + NORM_EPS)
    return (y * g.astype(jnp.float32)).astype(x.dtype)


def _split_points():
    pts, acc = [], 0
    for s in SPLIT_SIZES[:-1]:
        acc += s
        pts.append(acc)
    return pts


def _rope_tables(seq_len, dtype):
    freqs = 1.0 / (ROPE_THETA ** (jnp.arange(0, QK_ROPE_DIM, 2, dtype=jnp.float32) / QK_ROPE_DIM))
    ang = jnp.arange(seq_len, dtype=jnp.float32)[:, None] * freqs[None, :]
    return jnp.cos(ang).astype(dtype), jnp.sin(ang).astype(dtype)


def _apply_rope(x, cos, sin):
    x1, x2 = jnp.split(x, 2, axis=-1)
    return jnp.concatenate([x1 * cos - x2 * sin, x2 * cos + x1 * sin], axis=-1)


def _short_conv(u, w):
    up = jnp.pad(u, ((0, 0), (1, 1), (0, 0)))
    return up[:, :-2] * w[0] + up[:, 1:-1] * w[1] + up[:, 2:] * w[2]


def _attention(q, k, v):
    b, s, h, dqk = q.shape
    dv = v.shape[-1]
    nblk = s // Q_BLOCK
    scale = 1.0 / math.sqrt(dqk)
    qb = q.reshape(b, nblk, Q_BLOCK, h, dqk).transpose(1, 0, 2, 3, 4)

    def one_block(qblk):
        sc = jnp.einsum('bqhd,bkhd->bhqk', qblk, k, preferred_element_type=jnp.float32) * scale
        p = jax.nn.softmax(sc, axis=-1)
        return jnp.einsum('bhqk,bkhd->bqhd', p.astype(v.dtype), v)

    o = lax.map(one_block, qb)
    return o.transpose(1, 0, 2, 3, 4).reshape(b, s, h * dv)


def _layer(x, norm_pre, w_in, conv_w, q_norm, w_uq, kv_norm, w_ukv, w_out, norm_post):
    b, s, _ = x.shape
    hdn = _rmsnorm(x, norm_pre)
    proj = jnp.einsum('bsd,de->bse', hdn, w_in)
    u, b_gate, c_gate, z_conv, q_lat, kv_lat, k_pe, z_attn = jnp.split(proj, _split_points(), axis=-1)

    conv_out = b_gate * _short_conv(c_gate * u, conv_w) * jax.nn.silu(z_conv)

    cos, sin = _rope_tables(s, x.dtype)
    q = jnp.einsum('bsr,re->bse', _rmsnorm(q_lat, q_norm), w_uq).reshape(
        b, s, N_HEADS, QK_NOPE_DIM + QK_ROPE_DIM)
    q_nope, q_pe = jnp.split(q, [QK_NOPE_DIM], axis=-1)
    q_pe = _apply_rope(q_pe, cos[:, None, :], sin[:, None, :])
    kv = jnp.einsum('bsr,re->bse', _rmsnorm(kv_lat, kv_norm), w_ukv).reshape(
        b, s, N_HEADS, QK_NOPE_DIM + V_HEAD_DIM)
    k_nope, v = jnp.split(kv, [QK_NOPE_DIM], axis=-1)
    k_pe = _apply_rope(k_pe, cos, sin)
    k_pe = jnp.broadcast_to(k_pe[:, :, None, :], (b, s, N_HEADS, QK_ROPE_DIM))
    q_full = jnp.concatenate([q_nope, q_pe], axis=-1)
    k_full = jnp.concatenate([k_nope, k_pe], axis=-1)
    attn_out = _attention(q_full, k_full, v) * jax.nn.silu(z_attn)

    mix = jnp.concatenate([conv_out, attn_out], axis=-1)
    out = jnp.einsum('bse,ed->bsd', mix, w_out)
    return x + _rmsnorm(out, norm_post)


def setup_inputs(seed: int = 0) -> dict:
    key = jax.random.key(seed)
    ks = jax.random.split(key, 12)
    f32 = jnp.float32
    nrm = lambda k, shp, scale: (jax.random.normal(k, shp, f32) * scale)
    return {
        "x_prompt": nrm(ks[0], (BATCH, SEQ, D_MODEL), 1.0),
        "x_sample": nrm(ks[1], (DEC_BATCH, DEC_SEQ, D_MODEL), 1.0),
        "norm_pre": 1.0 + nrm(ks[2], (DEPTH, D_MODEL), 0.02),
        "w_in": nrm(ks[3], (DEPTH, D_MODEL, IN_WIDTH), D_MODEL ** -0.5),
        "conv_w": nrm(ks[4], (DEPTH, CONV_TAPS, CONV_WIDTH), CONV_TAPS ** -0.5),
        "q_norm": 1.0 + nrm(ks[5], (DEPTH, Q_LORA_RANK), 0.02),
        "w_uq": nrm(ks[6], (DEPTH, Q_LORA_RANK, N_HEADS * (QK_NOPE_DIM + QK_ROPE_DIM)), Q_LORA_RANK ** -0.5),
        "kv_norm": 1.0 + nrm(ks[7], (DEPTH, KV_LORA_RANK), 0.02),
        "w_ukv": nrm(ks[8], (DEPTH, KV_LORA_RANK, N_HEADS * (QK_NOPE_DIM + V_HEAD_DIM)), KV_LORA_RANK ** -0.5),
        "w_out": nrm(ks[9], (DEPTH, MIX_WIDTH, D_MODEL), MIX_WIDTH ** -0.5),
        "norm_post": 1.0 + nrm(ks[10], (DEPTH, D_MODEL), 0.02),
    }


def reference(x_prompt, x_sample, norm_pre, w_in, conv_w, q_norm, w_uq, kv_norm, w_ukv, w_out, norm_post):
    y_prompt = x_prompt
    y_sample = x_sample
    for l in range(DEPTH):
        y_prompt = _layer(y_prompt, norm_pre[l], w_in[l], conv_w[l], q_norm[l], w_uq[l],
                          kv_norm[l], w_ukv[l], w_out[l], norm_post[l])
        y_sample = _layer(y_sample, norm_pre[l], w_in[l], conv_w[l], q_norm[l], w_uq[l],
                          kv_norm[l], w_ukv[l], w_out[l], norm_post[l])
    return (y_prompt, y_sample)
```

```python
import functools
import math

import jax
import jax.numpy as jnp
from jax import lax
from jax.experimental import pallas as pl
from jax.experimental.pallas import tpu as pltpu

D_MODEL = 1024
CONV_WIDTH = 512
N_HEADS = 8
QK_NOPE_DIM = 64
QK_ROPE_DIM = 32
HALF_ROPE = QK_ROPE_DIM // 2
V_HEAD_DIM = 64
ATTN_WIDTH = N_HEADS * V_HEAD_DIM
Q_LORA_RANK = 384
KV_LORA_RANK = 256
ROPE_THETA = 10000.0
NORM_EPS = 1e-6
QK_DIM = QK_NOPE_DIM + QK_ROPE_DIM
HEAD_PAD = 128
V_ROWS = 80
LAT_WIDTH = Q_LORA_RANK + KV_LORA_RANK
OFF_QLAT = 4 * CONV_WIDTH
OFF_KVLAT = OFF_QLAT + Q_LORA_RANK
OFF_KPE = OFF_KVLAT + KV_LORA_RANK
OFF_ZATTN = OFF_KPE + QK_ROPE_DIM

BF16_SUBLANES = 16
VMEM_LIMIT_BYTES = 56 * 1024 * 1024

NT_DIMS = (((1,), (1,)), ((), ()))
TN_DIMS = (((0,), (0,)), ((), ()))


def _rms(x, g):
    return x * lax.rsqrt(jnp.mean(x * x, axis=-1, keepdims=True) + NORM_EPS) * g


def _silu(x):
    return x * (1.0 / (1.0 + jnp.exp(-x)))


def _dot(a, b):
    return jnp.dot(a, b, preferred_element_type=jnp.float32)


def _proj_kernel(x_ref, npre_ref, wconv_ref, wlat_ref, wzt_ref, wkpe_ref, qnorm_ref,
                 wuqt_ref, kvnorm_ref, wuk_ref, wuvt_ref, cos2_ref, sin2_ref, cost_ref,
                 sint_ref, cu_ref, g_ref, qt_ref, k_ref, vt_ref, gt_ref, *, q_scale):
    bf16 = jnp.bfloat16
    x = x_ref[0]
    tm = x.shape[0]
    hdn = _rms(x, npre_ref[...]).astype(bf16)

    pc = _dot(hdn, wconv_ref[...])
    cw = CONV_WIDTH
    cu_ref[0] = (pc[:, 2 * cw:3 * cw] * pc[:, 0:cw]).astype(bf16)
    g_ref[0] = (pc[:, cw:2 * cw] * _silu(pc[:, 3 * cw:4 * cw])).astype(bf16)

    lat = _dot(hdn, wlat_ref[...])
    qn = _rms(lat[:, :Q_LORA_RANK], qnorm_ref[...]).astype(bf16)
    kvn = _rms(lat[:, Q_LORA_RANK:], kvnorm_ref[...]).astype(bf16)

    qt = lax.dot_general(wuqt_ref[...], qn, NT_DIMS, preferred_element_type=jnp.float32)
    qt = qt * q_scale
    qt_ref[0] = qt.astype(bf16)
    cos_t = cost_ref[...]
    sin_t = sint_ref[...]
    for h in range(N_HEADS):
        r1 = h * HEAD_PAD + QK_NOPE_DIM
        r2 = r1 + HALF_ROPE
        x1 = qt[r1:r2]
        x2 = qt[r2:r2 + HALF_ROPE]
        qt_ref[0, r1:r2, :] = (x1 * cos_t - x2 * sin_t).astype(bf16)
        qt_ref[0, r2:r2 + HALF_ROPE, :] = (x2 * cos_t + x1 * sin_t).astype(bf16)

    kn = _dot(kvn, wuk_ref[...])
    kp = _dot(hdn, wkpe_ref[...])
    roped = kp[:, :HEAD_PAD] * cos2_ref[...] + kp[:, HEAD_PAD:] * sin2_ref[...]
    for h in range(N_HEADS):
        k_ref[0, h] = (kn[:, h * HEAD_PAD:(h + 1) * HEAD_PAD] + roped).astype(bf16)

    vt = lax.dot_general(wuvt_ref[...], kvn, NT_DIMS, preferred_element_type=jnp.float32)
    row = lax.broadcasted_iota(jnp.int32, (V_ROWS - V_HEAD_DIM, tm), 0)
    ones_rows = jnp.where(row == 0, 1.0, 0.0).astype(bf16)
    for h in range(N_HEADS):
        vt_ref[0, h, 0:V_HEAD_DIM, :] = vt[h * V_HEAD_DIM:(h + 1) * V_HEAD_DIM].astype(bf16)
        vt_ref[0, h, V_HEAD_DIM:V_ROWS, :] = ones_rows

    zt = lax.dot_general(wzt_ref[...], hdn, NT_DIMS, preferred_element_type=jnp.float32)
    gt_ref[0] = _silu(zt).astype(bf16)


def _proj_call(x, weights, tables, *, tm):
    b, s, d = x.shape
    bf16 = jnp.bfloat16
    grid = (b, s // tm)
    full = lambda arr: pl.BlockSpec(arr.shape, lambda bi, ti: (0,) * arr.ndim)
    (npre, wconv, wlat, wzt, wkpe, qnorm, wuqt, kvnorm, wuk, wuvt) = weights
    cos2, sin2, cos_t, sin_t = tables
    in_specs = [
        pl.BlockSpec((1, tm, d), lambda bi, ti: (bi, ti, 0)),
        full(npre), full(wconv), full(wlat), full(wzt), full(wkpe), full(qnorm),
        full(wuqt), full(kvnorm), full(wuk), full(wuvt),
        pl.BlockSpec((tm, HEAD_PAD), lambda bi, ti: (ti, 0)),
        pl.BlockSpec((tm, HEAD_PAD), lambda bi, ti: (ti, 0)),
        pl.BlockSpec((HALF_ROPE, tm), lambda bi, ti: (0, ti)),
        pl.BlockSpec((HALF_ROPE, tm), lambda bi, ti: (0, ti)),
    ]
    out_shape = [
        jax.ShapeDtypeStruct((b, s, CONV_WIDTH), bf16),
        jax.ShapeDtypeStruct((b, s, CONV_WIDTH), bf16),
        jax.ShapeDtypeStruct((b, N_HEADS * HEAD_PAD, s), bf16),
        jax.ShapeDtypeStruct((b, N_HEADS, s, HEAD_PAD), bf16),
        jax.ShapeDtypeStruct((b, N_HEADS, V_ROWS, s), bf16),
        jax.ShapeDtypeStruct((b, ATTN_WIDTH, s), bf16),
    ]
    out_specs = [
        pl.BlockSpec((1, tm, CONV_WIDTH), lambda bi, ti: (bi, ti, 0)),
        pl.BlockSpec((1, tm, CONV_WIDTH), lambda bi, ti: (bi, ti, 0)),
        pl.BlockSpec((1, N_HEADS * HEAD_PAD, tm), lambda bi, ti: (bi, 0, ti)),
        pl.BlockSpec((1, N_HEADS, tm, HEAD_PAD), lambda bi, ti: (bi, 0, ti, 0)),
        pl.BlockSpec((1, N_HEADS, V_ROWS, tm), lambda bi, ti: (bi, 0, 0, ti)),
        pl.BlockSpec((1, ATTN_WIDTH, tm), lambda bi, ti: (bi, 0, ti)),
    ]
    q_scale = math.log2(math.e) / math.sqrt(QK_DIM)
    return pl.pallas_call(
        functools.partial(_proj_kernel, q_scale=q_scale),
        grid=grid, in_specs=in_specs, out_specs=out_specs, out_shape=out_shape,
        compiler_params=pltpu.CompilerParams(
            dimension_semantics=("parallel", "parallel"),
            vmem_limit_bytes=VMEM_LIMIT_BYTES),
        name="proj",
    )(x, npre, wconv, wlat, wzt, wkpe, qnorm, wuqt, kvnorm, wuk, wuvt, cos2, sin2, cos_t, sin_t)


def _attn_kernel(qt_ref, k_ref, vt_ref, gt_ref, o_ref, *, bkv):
    qt = qt_ref[0]
    bq = qt.shape[1]
    s_len = k_ref.shape[2]
    nkv = s_len // bkv

    def body(i, carry):
        m, acc = carry
        off = pl.multiple_of(i * bkv, bkv)
        kc = k_ref[0, 0, pl.ds(off, bkv), :]
        sc = _dot(kc, qt)
        m_new = jnp.maximum(m, jnp.max(sc, axis=0, keepdims=True))
        p = jnp.exp2(sc - m_new).astype(jnp.bfloat16)
        alpha = jnp.exp2(m - m_new)
        vc = vt_ref[0, 0, :, pl.ds(off, bkv)]
        acc = alpha * acc + _dot(vc, p)
        return m_new, acc

    m0 = jnp.full((1, bq), -jnp.inf, jnp.float32)
    acc0 = jnp.zeros((V_ROWS, bq), jnp.float32)
    _, acc = lax.fori_loop(0, nkv, body, (m0, acc0))
    denom = acc[V_HEAD_DIM:V_HEAD_DIM + 1]
    out = acc[:V_HEAD_DIM] / denom
    o_ref[0] = (out * gt_ref[0].astype(jnp.float32)).astype(o_ref.dtype)


def _attn_call(qt, k, vt, gt, *, bq, bkv):
    b, _, s = qt.shape
    grid = (b, N_HEADS, s // bq)
    return pl.pallas_call(
        functools.partial(_attn_kernel, bkv=bkv),
        grid=grid,
        in_specs=[
            pl.BlockSpec((1, HEAD_PAD, bq), lambda bi, h, qi: (bi, h, qi)),
            pl.BlockSpec((1, 1, s, HEAD_PAD), lambda bi, h, qi: (bi, h, 0, 0)),
            pl.BlockSpec((1, 1, V_ROWS, s), lambda bi, h, qi: (bi, h, 0, 0)),
            pl.BlockSpec((1, V_HEAD_DIM, bq), lambda bi, h, qi: (bi, h, qi)),
        ],
        out_specs=pl.BlockSpec((1, V_HEAD_DIM, bq), lambda bi, h, qi: (bi, h, qi)),
        out_shape=jax.ShapeDtypeStruct((b, ATTN_WIDTH, s), jnp.bfloat16),
        compiler_params=pltpu.CompilerParams(
            dimension_semantics=("parallel", "parallel", "arbitrary"),
            vmem_limit_bytes=VMEM_LIMIT_BYTES),
        name="attn",
    )(qt, k, vt, gt)


def _out_kernel(x_ref, cu_ref, cup_ref, cun_ref, g_ref, at_ref, convw_ref, wa_ref, wb_ref,
                npost_ref, y_ref):
    ti = pl.program_id(1)
    nt = pl.num_programs(1)
    cu = cu_ref[0].astype(jnp.float32)
    tm = cu.shape[0]
    prev_row = cup_ref[0, BF16_SUBLANES - 1:BF16_SUBLANES, :].astype(jnp.float32)
    next_row = cun_ref[0, 0:1, :].astype(jnp.float32)
    prev_row = jnp.where(ti > 0, prev_row, 0.0)
    next_row = jnp.where(ti < nt - 1, next_row, 0.0)
    row = lax.broadcasted_iota(jnp.int32, cu.shape, 0)
    up = jnp.where(row == 0, prev_row, pltpu.roll(cu, 1, 0))
    dn = jnp.where(row == tm - 1, next_row, pltpu.roll(cu, tm - 1, 0))
    w = convw_ref[...]
    conv = up * w[0:1] + cu * w[1:2] + dn * w[2:3]
    conv_out = (conv * g_ref[0].astype(jnp.float32)).astype(jnp.bfloat16)
    out = _dot(conv_out, wa_ref[...])
    out = out + lax.dot_general(at_ref[0], wb_ref[...], TN_DIMS,
                                preferred_element_type=jnp.float32)
    y_ref[0] = x_ref[0] + _rms(out, npost_ref[...])


def _out_call(x, cu, g, at, conv_w, wa, wb, npost, *, tm):
    b, s, d = x.shape
    grid = (b, s // tm)
    halo = BF16_SUBLANES
    nb = tm // halo
    last = s // halo - 1
    full = lambda arr: pl.BlockSpec(arr.shape, lambda bi, ti: (0,) * arr.ndim)
    return pl.pallas_call(
        _out_kernel,
        grid=grid,
        in_specs=[
            pl.BlockSpec((1, tm, d), lambda bi, ti: (bi, ti, 0)),
            pl.BlockSpec((1, tm, CONV_WIDTH), lambda bi, ti: (bi, ti, 0)),
            pl.BlockSpec((1, halo, CONV_WIDTH),
                         lambda bi, ti: (bi, jnp.maximum(ti * nb - 1, 0), 0)),
            pl.BlockSpec((1, halo, CONV_WIDTH),
                         lambda bi, ti: (bi, jnp.minimum((ti + 1) * nb, last), 0)),
            pl.BlockSpec((1, tm, CONV_WIDTH), lambda bi, ti: (bi, ti, 0)),
            pl.BlockSpec((1, ATTN_WIDTH, tm), lambda bi, ti: (bi, 0, ti)),
            full(conv_w), full(wa), full(wb), full(npost),
        ],
        out_specs=pl.BlockSpec((1, tm, d), lambda bi, ti: (bi, ti, 0)),
        out_shape=jax.ShapeDtypeStruct((b, s, d), x.dtype),
        compiler_params=pltpu.CompilerParams(
            dimension_semantics=("parallel", "parallel"),
            vmem_limit_bytes=VMEM_LIMIT_BYTES),
        name="outproj",
    )(x, cu, cu, cu, g, at, conv_w, wa, wb, npost)


def _prep_weights(norm_pre, w_in, q_norm, w_uq, kv_norm, w_ukv):
    bf16 = jnp.bfloat16
    wconv = w_in[:, :OFF_QLAT].astype(bf16)
    wlat = w_in[:, OFF_QLAT:OFF_KPE].astype(bf16)
    wzt = w_in[:, OFF_ZATTN:].T.astype(bf16)
    kpe = w_in[:, OFF_KPE:OFF_ZATTN]
    kpe_sw = jnp.concatenate([kpe[:, HALF_ROPE:], kpe[:, :HALF_ROPE]], axis=1)
    pad_l = jnp.zeros((D_MODEL, QK_NOPE_DIM), w_in.dtype)
    pad_r = jnp.zeros((D_MODEL, HEAD_PAD - QK_DIM), w_in.dtype)
    wkpe = jnp.concatenate([pad_l, kpe, pad_r, pad_l, kpe_sw, pad_r], axis=1).astype(bf16)
    wuq = w_uq.reshape(Q_LORA_RANK, N_HEADS, QK_DIM)
    wuq = jnp.pad(wuq, ((0, 0), (0, 0), (0, HEAD_PAD - QK_DIM)))
    wuqt = wuq.reshape(Q_LORA_RANK, N_HEADS * HEAD_PAD).T.astype(bf16)
    wukv = w_ukv.reshape(KV_LORA_RANK, N_HEADS, QK_NOPE_DIM + V_HEAD_DIM)
    wuk = jnp.pad(wukv[:, :, :QK_NOPE_DIM], ((0, 0), (0, 0), (0, HEAD_PAD - QK_NOPE_DIM)))
    wuk = wuk.reshape(KV_LORA_RANK, N_HEADS * HEAD_PAD).astype(bf16)
    wuvt = wukv[:, :, QK_NOPE_DIM:].reshape(KV_LORA_RANK, ATTN_WIDTH).T.astype(bf16)
    return (norm_pre.reshape(1, -1), wconv, wlat, wzt, wkpe, q_norm.reshape(1, -1), wuqt,
            kv_norm.reshape(1, -1), wuk, wuvt)


def _rope_tables(seq_len):
    freqs = 1.0 / (ROPE_THETA ** (jnp.arange(0, QK_ROPE_DIM, 2, dtype=jnp.float32) / QK_ROPE_DIM))
    ang = jnp.arange(seq_len, dtype=jnp.float32)[:, None] * freqs[None, :]
    cos, sin = jnp.cos(ang), jnp.sin(ang)
    zl = jnp.zeros((seq_len, QK_NOPE_DIM), jnp.float32)
    zr = jnp.zeros((seq_len, HEAD_PAD - QK_DIM), jnp.float32)
    cos2 = jnp.concatenate([zl, cos, cos, zr], axis=1)
    sin2 = jnp.concatenate([zl, -sin, sin, zr], axis=1)
    return cos2, sin2, cos.T, sin.T


def _layer(x, weights, conv_w, wa, wb, npost, *, tm_proj, bq, bkv, tm_out):
    tables = _rope_tables(x.shape[1])
    cu, g, qt, k, vt, gt = _proj_call(x, weights, tables, tm=tm_proj)
    at = _attn_call(qt, k, vt, gt, bq=bq, bkv=bkv)
    return _out_call(x, cu, g, at, conv_w, wa, wb, npost, tm=tm_out)


def kernel(x_prompt, x_sample, norm_pre, w_in, conv_w, q_norm, w_uq, kv_norm, w_ukv, w_out, norm_post):
    y_prompt, y_sample = x_prompt, x_sample
    for l in range(norm_pre.shape[0]):
        weights = _prep_weights(norm_pre[l], w_in[l], q_norm[l], w_uq[l], kv_norm[l], w_ukv[l])
        wa = w_out[l, :CONV_WIDTH].astype(jnp.bfloat16)
        wb = w_out[l, CONV_WIDTH:].astype(jnp.bfloat16)
        npost = norm_post[l].reshape(1, -1)
        cfg = dict(tm_proj=256, bq=256, bkv=512, tm_out=512)
        y_prompt = _layer(y_prompt, weights, conv_w[l], wa, wb, npost, **cfg)
        y_sample = _layer(y_sample, weights, conv_w[l], wa, wb, npost, **cfg)
    return (y_prompt, y_sample)
```

```python
import functools
import math

import jax
import jax.numpy as jnp
from jax import lax
from jax.experimental import pallas as pl
from jax.experimental.pallas import tpu as pltpu

D_MODEL = 1024
CONV_WIDTH = 512
N_HEADS = 8
QK_NOPE_DIM = 64
QK_ROPE_DIM = 32
HALF_ROPE = QK_ROPE_DIM // 2
V_HEAD_DIM = 64
ATTN_WIDTH = N_HEADS * V_HEAD_DIM
Q_LORA_RANK = 384
KV_LORA_RANK = 256
ROPE_THETA = 10000.0
NORM_EPS = 1e-6
QK_DIM = QK_NOPE_DIM + QK_ROPE_DIM
HEAD_PAD = 128
V_ROWS = 80
LAT_WIDTH = Q_LORA_RANK + KV_LORA_RANK
OFF_QLAT = 4 * CONV_WIDTH
OFF_KVLAT = OFF_QLAT + Q_LORA_RANK
OFF_KPE = OFF_KVLAT + KV_LORA_RANK
OFF_ZATTN = OFF_KPE + QK_ROPE_DIM

BF16_SUBLANES = 16
VMEM_LIMIT_BYTES = 56 * 1024 * 1024

NT_DIMS = (((1,), (1,)), ((), ()))
TN_DIMS = (((0,), (0,)), ((), ()))


def _rms(x, g):
    return x * lax.rsqrt(jnp.mean(x * x, axis=-1, keepdims=True) + NORM_EPS) * g


def _silu(x):
    return x * (1.0 / (1.0 + jnp.exp(-x)))


def _dot(a, b):
    return jnp.dot(a, b, preferred_element_type=jnp.float32)


def _proj_kernel(x_ref, npre_ref, wconv_ref, wlat_ref, wzt_ref, wkpe_ref, qnorm_ref,
                 wuqt_ref, kvnorm_ref, wuk_ref, wuvt_ref, cos2_ref, sin2_ref, cost_ref,
                 sint_ref, cu_ref, g_ref, qt_ref, k_ref, vt_ref, gt_ref, *, q_scale):
    bf16 = jnp.bfloat16
    x = x_ref[0]
    tm = x.shape[0]
    hdn = _rms(x, npre_ref[...]).astype(bf16)

    pc = _dot(hdn, wconv_ref[...])
    cw = CONV_WIDTH
    cu_ref[0] = (pc[:, 2 * cw:3 * cw] * pc[:, 0:cw]).astype(bf16)
    g_ref[0] = (pc[:, cw:2 * cw] * _silu(pc[:, 3 * cw:4 * cw])).astype(bf16)

    lat = _dot(hdn, wlat_ref[...])
    qn = _rms(lat[:, :Q_LORA_RANK], qnorm_ref[...]).astype(bf16)
    kvn = _rms(lat[:, Q_LORA_RANK:], kvnorm_ref[...]).astype(bf16)

    qt = lax.dot_general(wuqt_ref[...], qn, NT_DIMS, preferred_element_type=jnp.float32)
    qt = qt * q_scale
    qt_ref[0] = qt.astype(bf16)
    cos_t = cost_ref[...]
    sin_t = sint_ref[...]
    for h in range(N_HEADS):
        r1 = h * HEAD_PAD + QK_NOPE_DIM
        r2 = r1 + HALF_ROPE
        x1 = qt[r1:r2]
        x2 = qt[r2:r2 + HALF_ROPE]
        qt_ref[0, r1:r2, :] = (x1 * cos_t - x2 * sin_t).astype(bf16)
        qt_ref[0, r2:r2 + HALF_ROPE, :] = (x2 * cos_t + x1 * sin_t).astype(bf16)

    kn = _dot(kvn, wuk_ref[...])
    kp = _dot(hdn, wkpe_ref[...])
    roped = kp[:, :HEAD_PAD] * cos2_ref[...] + kp[:, HEAD_PAD:] * sin2_ref[...]
    for h in range(N_HEADS):
        k_ref[0, h] = (kn[:, h * HEAD_PAD:(h + 1) * HEAD_PAD] + roped).astype(bf16)

    vt = lax.dot_general(wuvt_ref[...], kvn, NT_DIMS, preferred_element_type=jnp.float32)
    row = lax.broadcasted_iota(jnp.int32, (V_ROWS - V_HEAD_DIM, tm), 0)
    ones_rows = jnp.where(row == 0, 1.0, 0.0).astype(bf16)
    for h in range(N_HEADS):
        vt_ref[0, h, 0:V_HEAD_DIM, :] = vt[h * V_HEAD_DIM:(h + 1) * V_HEAD_DIM].astype(bf16)
        vt_ref[0, h, V_HEAD_DIM:V_ROWS, :] = ones_rows

    zt = lax.dot_general(wzt_ref[...], hdn, NT_DIMS, preferred_element_type=jnp.float32)
    gt_ref[0] = _silu(zt).astype(bf16)


def _proj_call(x, weights, tables, *, tm):
    b, s, d = x.shape
    bf16 = jnp.bfloat16
    grid = (b, s // tm)
    full = lambda arr: pl.BlockSpec(arr.shape, lambda bi, ti: (0,) * arr.ndim)
    (npre, wconv, wlat, wzt, wkpe, qnorm, wuqt, kvnorm, wuk, wuvt) = weights
    cos2, sin2, cos_t, sin_t = tables
    in_specs = [
        pl.BlockSpec((1, tm, d), lambda bi, ti: (bi, ti, 0)),
        full(npre), full(wconv), full(wlat), full(wzt), full(wkpe), full(qnorm),
        full(wuqt), full(kvnorm), full(wuk), full(wuvt),
        pl.BlockSpec((tm, HEAD_PAD), lambda bi, ti: (ti, 0)),
        pl.BlockSpec((tm, HEAD_PAD), lambda bi, ti: (ti, 0)),
        pl.BlockSpec((HALF_ROPE, tm), lambda bi, ti: (0, ti)),
        pl.BlockSpec((HALF_ROPE, tm), lambda bi, ti: (0, ti)),
    ]
    out_shape = [
        jax.ShapeDtypeStruct((b, s, CONV_WIDTH), bf16),
        jax.ShapeDtypeStruct((b, s, CONV_WIDTH), bf16),
        jax.ShapeDtypeStruct((b, N_HEADS * HEAD_PAD, s), bf16),
        jax.ShapeDtypeStruct((b, N_HEADS, s, HEAD_PAD), bf16),
        jax.ShapeDtypeStruct((b, N_HEADS, V_ROWS, s), bf16),
        jax.ShapeDtypeStruct((b, ATTN_WIDTH, s), bf16),
    ]
    out_specs = [
        pl.BlockSpec((1, tm, CONV_WIDTH), lambda bi, ti: (bi, ti, 0)),
        pl.BlockSpec((1, tm, CONV_WIDTH), lambda bi, ti: (bi, ti, 0)),
        pl.BlockSpec((1, N_HEADS * HEAD_PAD, tm), lambda bi, ti: (bi, 0, ti)),
        pl.BlockSpec((1, N_HEADS, tm, HEAD_PAD), lambda bi, ti: (bi, 0, ti, 0)),
        pl.BlockSpec((1, N_HEADS, V_ROWS, tm), lambda bi, ti: (bi, 0, 0, ti)),
        pl.BlockSpec((1, ATTN_WIDTH, tm), lambda bi, ti: (bi, 0, ti)),
    ]
    q_scale = math.log2(math.e) / math.sqrt(QK_DIM)
    return pl.pallas_call(
        functools.partial(_proj_kernel, q_scale=q_scale),
        grid=grid, in_specs=in_specs, out_specs=out_specs, out_shape=out_shape,
        compiler_params=pltpu.CompilerParams(
            dimension_semantics=("parallel", "parallel"),
            vmem_limit_bytes=VMEM_LIMIT_BYTES),
        name="proj",
    )(x, npre, wconv, wlat, wzt, wkpe, qnorm, wuqt, kvnorm, wuk, wuvt, cos2, sin2, cos_t, sin_t)


def _attn_kernel(qt_ref, k_ref, vt_ref, gt_ref, o_ref, s_a, s_b, acc_ref, *, bq, bkv):
    s_scr = (s_a, s_b)
    s_len = k_ref.shape[2]
    nkv = s_len // bkv
    ntrips = (s_len // bq) * nkv
    assert nkv & (nkv - 1) == 0
    kv_shift = nkv.bit_length() - 1

    def scores(n, bank):
        n = jnp.minimum(n, ntrips - 1)
        koff = pl.multiple_of((n & (nkv - 1)) * bkv, bkv)
        qoff = pl.multiple_of((n >> kv_shift) * bq, bq)
        sc = _dot(k_ref[0, 0, pl.ds(koff, bkv), :], qt_ref[0, :, pl.ds(qoff, bq)])
        s_scr[bank][...] = sc
        return jnp.max(sc, axis=0, keepdims=True)

    def softmax_values(n, bank, m, cmax):
        chunk = n & (nkv - 1)
        m = jnp.where(chunk == 0, -jnp.inf, m)
        m_new = jnp.maximum(m, cmax)
        p = jnp.exp2(s_scr[bank][...] - m_new).astype(jnp.bfloat16)
        alpha = jnp.exp2(m - m_new)
        koff = pl.multiple_of(chunk * bkv, bkv)
        acc = alpha * acc_ref[...] + _dot(vt_ref[0, 0, :, pl.ds(koff, bkv)], p)
        acc_ref[...] = acc

        @pl.when(chunk == nkv - 1)
        def _():
            qoff = pl.multiple_of((n >> kv_shift) * bq, bq)
            denom = acc[V_HEAD_DIM:V_HEAD_DIM + 1]
            gate = gt_ref[0, :, pl.ds(qoff, bq)].astype(jnp.float32)
            out = acc[:V_HEAD_DIM] / denom * gate
            o_ref[0, :, pl.ds(qoff, bq)] = out.astype(o_ref.dtype)

        return m_new

    def trip(n, bank, carry):
        m, cmax = carry
        cmax_next = scores(n + 1, 1 - bank)
        return softmax_values(n, bank, m, cmax), cmax_next

    acc_ref[...] = jnp.zeros(acc_ref.shape, acc_ref.dtype)
    carry = (jnp.zeros((1, bq), jnp.float32), scores(0, 0))

    def steady(n, carry):
        return lax.cond((n & 1) == 1, lambda c: trip(n, 1, c), lambda c: trip(n, 0, c), carry)

    lax.fori_loop(0, ntrips, steady, carry)


def _attn_call(qt, k, vt, gt, *, bq, bkv):
    b, _, s = qt.shape
    grid = (b, N_HEADS)
    return pl.pallas_call(
        functools.partial(_attn_kernel, bq=bq, bkv=bkv),
        grid=grid,
        in_specs=[
            pl.BlockSpec((1, HEAD_PAD, s), lambda bi, h: (bi, h, 0)),
            pl.BlockSpec((1, 1, s, HEAD_PAD), lambda bi, h: (bi, h, 0, 0)),
            pl.BlockSpec((1, 1, V_ROWS, s), lambda bi, h: (bi, h, 0, 0)),
            pl.BlockSpec((1, V_HEAD_DIM, s), lambda bi, h: (bi, h, 0)),
        ],
        out_specs=pl.BlockSpec((1, V_HEAD_DIM, s), lambda bi, h: (bi, h, 0)),
        out_shape=jax.ShapeDtypeStruct((b, ATTN_WIDTH, s), jnp.bfloat16),
        scratch_shapes=[pltpu.VMEM((bkv, bq), jnp.float32)] * 2
        + [pltpu.VMEM((V_ROWS, bq), jnp.float32)],
        compiler_params=pltpu.CompilerParams(
            dimension_semantics=("parallel", "parallel"),
            vmem_limit_bytes=VMEM_LIMIT_BYTES),
        name="attn",
    )(qt, k, vt, gt)


def _out_kernel(x_ref, cu_ref, cup_ref, cun_ref, g_ref, at_ref, convw_ref, wa_ref, wb_ref,
                npost_ref, y_ref):
    ti = pl.program_id(1)
    nt = pl.num_programs(1)
    cu = cu_ref[0].astype(jnp.float32)
    tm = cu.shape[0]
    prev_row = cup_ref[0, BF16_SUBLANES - 1:BF16_SUBLANES, :].astype(jnp.float32)
    next_row = cun_ref[0, 0:1, :].astype(jnp.float32)
    prev_row = jnp.where(ti > 0, prev_row, 0.0)
    next_row = jnp.where(ti < nt - 1, next_row, 0.0)
    row = lax.broadcasted_iota(jnp.int32, cu.shape, 0)
    up = jnp.where(row == 0, prev_row, pltpu.roll(cu, 1, 0))
    dn = jnp.where(row == tm - 1, next_row, pltpu.roll(cu, tm - 1, 0))
    w = convw_ref[...]
    conv = up * w[0:1] + cu * w[1:2] + dn * w[2:3]
    conv_out = (conv * g_ref[0].astype(jnp.float32)).astype(jnp.bfloat16)
    out = _dot(conv_out, wa_ref[...])
    out = out + lax.dot_general(at_ref[0], wb_ref[...], TN_DIMS,
                                preferred_element_type=jnp.float32)
    y_ref[0] = x_ref[0] + _rms(out, npost_ref[...])


def _out_call(x, cu, g, at, conv_w, wa, wb, npost, *, tm):
    b, s, d = x.shape
    grid = (b, s // tm)
    halo = BF16_SUBLANES
    nb = tm // halo
    last = s // halo - 1
    full = lambda arr: pl.BlockSpec(arr.shape, lambda bi, ti: (0,) * arr.ndim)
    return pl.pallas_call(
        _out_kernel,
        grid=grid,
        in_specs=[
            pl.BlockSpec((1, tm, d), lambda bi, ti: (bi, ti, 0)),
            pl.BlockSpec((1, tm, CONV_WIDTH), lambda bi, ti: (bi, ti, 0)),
            pl.BlockSpec((1, halo, CONV_WIDTH),
                         lambda bi, ti: (bi, jnp.maximum(ti * nb - 1, 0), 0)),
            pl.BlockSpec((1, halo, CONV_WIDTH),
                         lambda bi, ti: (bi, jnp.minimum((ti + 1) * nb, last), 0)),
            pl.BlockSpec((1, tm, CONV_WIDTH), lambda bi, ti: (bi, ti, 0)),
            pl.BlockSpec((1, ATTN_WIDTH, tm), lambda bi, ti: (bi, 0, ti)),
            full(conv_w), full(wa), full(wb), full(npost),
        ],
        out_specs=pl.BlockSpec((1, tm, d), lambda bi, ti: (bi, ti, 0)),
        out_shape=jax.ShapeDtypeStruct((b, s, d), x.dtype),
        compiler_params=pltpu.CompilerParams(
            dimension_semantics=("parallel", "parallel"),
            vmem_limit_bytes=VMEM_LIMIT_BYTES),
        name="outproj",
    )(x, cu, cu, cu, g, at, conv_w, wa, wb, npost)


def _prep_weights(norm_pre, w_in, q_norm, w_uq, kv_norm, w_ukv):
    bf16 = jnp.bfloat16
    wconv = w_in[:, :OFF_QLAT].astype(bf16)
    wlat = w_in[:, OFF_QLAT:OFF_KPE].astype(bf16)
    wzt = w_in[:, OFF_ZATTN:].T.astype(bf16)
    kpe = w_in[:, OFF_KPE:OFF_ZATTN]
    kpe_sw = jnp.concatenate([kpe[:, HALF_ROPE:], kpe[:, :HALF_ROPE]], axis=1)
    pad_l = jnp.zeros((D_MODEL, QK_NOPE_DIM), w_in.dtype)
    pad_r = jnp.zeros((D_MODEL, HEAD_PAD - QK_DIM), w_in.dtype)
    wkpe = jnp.concatenate([pad_l, kpe, pad_r, pad_l, kpe_sw, pad_r], axis=1).astype(bf16)
    wuq = w_uq.reshape(Q_LORA_RANK, N_HEADS, QK_DIM)
    wuq = jnp.pad(wuq, ((0, 0), (0, 0), (0, HEAD_PAD - QK_DIM)))
    wuqt = wuq.reshape(Q_LORA_RANK, N_HEADS * HEAD_PAD).T.astype(bf16)
    wukv = w_ukv.reshape(KV_LORA_RANK, N_HEADS, QK_NOPE_DIM + V_HEAD_DIM)
    wuk = jnp.pad(wukv[:, :, :QK_NOPE_DIM], ((0, 0), (0, 0), (0, HEAD_PAD - QK_NOPE_DIM)))
    wuk = wuk.reshape(KV_LORA_RANK, N_HEADS * HEAD_PAD).astype(bf16)
    wuvt = wukv[:, :, QK_NOPE_DIM:].reshape(KV_LORA_RANK, ATTN_WIDTH).T.astype(bf16)
    return (norm_pre.reshape(1, -1), wconv, wlat, wzt, wkpe, q_norm.reshape(1, -1), wuqt,
            kv_norm.reshape(1, -1), wuk, wuvt)


def _rope_tables(seq_len):
    freqs = 1.0 / (ROPE_THETA ** (jnp.arange(0, QK_ROPE_DIM, 2, dtype=jnp.float32) / QK_ROPE_DIM))
    ang = jnp.arange(seq_len, dtype=jnp.float32)[:, None] * freqs[None, :]
    cos, sin = jnp.cos(ang), jnp.sin(ang)
    zl = jnp.zeros((seq_len, QK_NOPE_DIM), jnp.float32)
    zr = jnp.zeros((seq_len, HEAD_PAD - QK_DIM), jnp.float32)
    cos2 = jnp.concatenate([zl, cos, cos, zr], axis=1)
    sin2 = jnp.concatenate([zl, -sin, sin, zr], axis=1)
    return cos2, sin2, cos.T, sin.T


def _layer(x, weights, conv_w, wa, wb, npost, *, tm_proj, bq, bkv, tm_out):
    tables = _rope_tables(x.shape[1])
    cu, g, qt, k, vt, gt = _proj_call(x, weights, tables, tm=tm_proj)
    at = _attn_call(qt, k, vt, gt, bq=bq, bkv=min(bkv, x.shape[1]))
    return _out_call(x, cu, g, at, conv_w, wa, wb, npost, tm=tm_out)


def kernel(x_prompt, x_sample, norm_pre, w_in, conv_w, q_norm, w_uq, kv_norm, w_ukv, w_out, norm_post):
    y_prompt, y_sample = x_prompt, x_sample
    for l in range(norm_pre.shape[0]):
        weights = _prep_weights(norm_pre[l], w_in[l], q_norm[l], w_uq[l], kv_norm[l], w_ukv[l])
        wa = w_out[l, :CONV_WIDTH].astype(jnp.bfloat16)
        wb = w_out[l, CONV_WIDTH:].astype(jnp.bfloat16)
        npost = norm_post[l].reshape(1, -1)
        cfg = dict(tm_proj=256, bq=256, bkv=2048, tm_out=512)
        y_prompt = _layer(y_prompt, weights, conv_w[l], wa, wb, npost, **cfg)
        y_sample = _layer(y_sample, weights, conv_w[l], wa, wb, npost, **cfg)
    return (y_prompt, y_sample)
```

```python
import functools
import math

import jax
import jax.numpy as jnp
from jax import lax
from jax.experimental import pallas as pl
from jax.experimental.pallas import tpu as pltpu

D_MODEL = 1024
CONV_WIDTH = 512
N_HEADS = 8
QK_NOPE_DIM = 64
QK_ROPE_DIM = 32
HALF_ROPE = QK_ROPE_DIM // 2
V_HEAD_DIM = 64
ATTN_WIDTH = N_HEADS * V_HEAD_DIM
Q_LORA_RANK = 384
KV_LORA_RANK = 256
ROPE_THETA = 10000.0
NORM_EPS = 1e-6
QK_DIM = QK_NOPE_DIM + QK_ROPE_DIM
HEAD_PAD = 128
V_ROWS = 80
LAT_WIDTH = Q_LORA_RANK + KV_LORA_RANK
OFF_QLAT = 4 * CONV_WIDTH
OFF_KVLAT = OFF_QLAT + Q_LORA_RANK
OFF_KPE = OFF_KVLAT + KV_LORA_RANK
OFF_ZATTN = OFF_KPE + QK_ROPE_DIM

BF16_SUBLANES = 16
VMEM_LIMIT_BYTES = 56 * 1024 * 1024

NT_DIMS = (((1,), (1,)), ((), ()))
TN_DIMS = (((0,), (0,)), ((), ()))


def _rms(x, g):
    return x * lax.rsqrt(jnp.mean(x * x, axis=-1, keepdims=True) + NORM_EPS) * g


def _silu(x):
    return x * (1.0 / (1.0 + jnp.exp(-x)))


def _dot(a, b):
    return jnp.dot(a, b, preferred_element_type=jnp.float32)


def _proj_kernel(x_ref, npre_ref, wconv_ref, wlat_ref, wzt_ref, wkpe_ref, qnorm_ref,
                 wuqt_ref, kvnorm_ref, wuk_ref, wuvt_ref, cos2_ref, sin2_ref, cost_ref,
                 sint_ref, cu_ref, g_ref, qt_ref, k_ref, vt_ref, gt_ref, *, q_scale):
    bf16 = jnp.bfloat16
    x = x_ref[0]
    tm = x.shape[0]
    hdn = _rms(x, npre_ref[...]).astype(bf16)

    pc = _dot(hdn, wconv_ref[...])
    cw = CONV_WIDTH
    cu_ref[0] = (pc[:, 2 * cw:3 * cw] * pc[:, 0:cw]).astype(bf16)
    g_ref[0] = (pc[:, cw:2 * cw] * _silu(pc[:, 3 * cw:4 * cw])).astype(bf16)

    lat = _dot(hdn, wlat_ref[...])
    qn = _rms(lat[:, :Q_LORA_RANK], qnorm_ref[...]).astype(bf16)
    kvn = _rms(lat[:, Q_LORA_RANK:], kvnorm_ref[...]).astype(bf16)

    qt = lax.dot_general(wuqt_ref[...], qn, NT_DIMS, preferred_element_type=jnp.float32)
    qt = qt * q_scale
    qt_ref[0] = qt.astype(bf16)
    cos_t = cost_ref[...]
    sin_t = sint_ref[...]
    for h in range(N_HEADS):
        r1 = h * HEAD_PAD + QK_NOPE_DIM
        r2 = r1 + HALF_ROPE
        x1 = qt[r1:r2]
        x2 = qt[r2:r2 + HALF_ROPE]
        qt_ref[0, r1:r2, :] = (x1 * cos_t - x2 * sin_t).astype(bf16)
        qt_ref[0, r2:r2 + HALF_ROPE, :] = (x2 * cos_t + x1 * sin_t).astype(bf16)

    kn = _dot(kvn, wuk_ref[...])
    kp = _dot(hdn, wkpe_ref[...])
    roped = kp[:, :HEAD_PAD] * cos2_ref[...] + kp[:, HEAD_PAD:] * sin2_ref[...]
    for h in range(N_HEADS):
        k_ref[0, h] = (kn[:, h * HEAD_PAD:(h + 1) * HEAD_PAD] + roped).astype(bf16)

    vt = lax.dot_general(wuvt_ref[...], kvn, NT_DIMS, preferred_element_type=jnp.float32)
    row = lax.broadcasted_iota(jnp.int32, (V_ROWS - V_HEAD_DIM, tm), 0)
    ones_rows = jnp.where(row == 0, 1.0, 0.0).astype(bf16)
    for h in range(N_HEADS):
        vt_ref[0, h, 0:V_HEAD_DIM, :] = vt[h * V_HEAD_DIM:(h + 1) * V_HEAD_DIM].astype(bf16)
        vt_ref[0, h, V_HEAD_DIM:V_ROWS, :] = ones_rows

    zt = lax.dot_general(wzt_ref[...], hdn, NT_DIMS, preferred_element_type=jnp.float32)
    gt_ref[0] = _silu(zt).astype(bf16)


def _proj_call(x, weights, tables, *, tm):
    b, s, d = x.shape
    bf16 = jnp.bfloat16
    grid = (b, s // tm)
    full = lambda arr: pl.BlockSpec(arr.shape, lambda bi, ti: (0,) * arr.ndim)
    (npre, wconv, wlat, wzt, wkpe, qnorm, wuqt, kvnorm, wuk, wuvt) = weights
    cos2, sin2, cos_t, sin_t = tables
    in_specs = [
        pl.BlockSpec((1, tm, d), lambda bi, ti: (bi, ti, 0)),
        full(npre), full(wconv), full(wlat), full(wzt), full(wkpe), full(qnorm),
        full(wuqt), full(kvnorm), full(wuk), full(wuvt),
        pl.BlockSpec((tm, HEAD_PAD), lambda bi, ti: (ti, 0)),
        pl.BlockSpec((tm, HEAD_PAD), lambda bi, ti: (ti, 0)),
        pl.BlockSpec((HALF_ROPE, tm), lambda bi, ti: (0, ti)),
        pl.BlockSpec((HALF_ROPE, tm), lambda bi, ti: (0, ti)),
    ]
    out_shape = [
        jax.ShapeDtypeStruct((b, s, CONV_WIDTH), bf16),
        jax.ShapeDtypeStruct((b, s, CONV_WIDTH), bf16),
        jax.ShapeDtypeStruct((b, N_HEADS * HEAD_PAD, s), bf16),
        jax.ShapeDtypeStruct((b, N_HEADS, s, HEAD_PAD), bf16),
        jax.ShapeDtypeStruct((b, N_HEADS, V_ROWS, s), bf16),
        jax.ShapeDtypeStruct((b, ATTN_WIDTH, s), bf16),
    ]
    out_specs = [
        pl.BlockSpec((1, tm, CONV_WIDTH), lambda bi, ti: (bi, ti, 0)),
        pl.BlockSpec((1, tm, CONV_WIDTH), lambda bi, ti: (bi, ti, 0)),
        pl.BlockSpec((1, N_HEADS * HEAD_PAD, tm), lambda bi, ti: (bi, 0, ti)),
        pl.BlockSpec((1, N_HEADS, tm, HEAD_PAD), lambda bi, ti: (bi, 0, ti, 0)),
        pl.BlockSpec((1, N_HEADS, V_ROWS, tm), lambda bi, ti: (bi, 0, 0, ti)),
        pl.BlockSpec((1, ATTN_WIDTH, tm), lambda bi, ti: (bi, 0, ti)),
    ]
    q_scale = math.log2(math.e) / math.sqrt(QK_DIM)
    return pl.pallas_call(
        functools.partial(_proj_kernel, q_scale=q_scale),
        grid=grid, in_specs=in_specs, out_specs=out_specs, out_shape=out_shape,
        compiler_params=pltpu.CompilerParams(
            dimension_semantics=("parallel", "parallel"),
            vmem_limit_bytes=VMEM_LIMIT_BYTES),
        name="proj",
    )(x, npre, wconv, wlat, wzt, wkpe, qnorm, wuqt, kvnorm, wuk, wuvt, cos2, sin2, cos_t, sin_t)


def _attn_kernel(qt_ref, k_ref, vt_ref, gt_ref, o_ref, s_a, s_b, acc_ref, *, bq, bkv, sub):
    s_scr = (s_a, s_b)
    s_len = k_ref.shape[2]
    nkv = s_len // bkv
    ntrips = (s_len // bq) * nkv
    assert nkv & (nkv - 1) == 0
    kv_shift = nkv.bit_length() - 1

    def scores(n, bank, j):
        n = jnp.minimum(n, ntrips - 1)
        koff = pl.multiple_of((n & (nkv - 1)) * bkv + j * sub, sub)
        qoff = pl.multiple_of((n >> kv_shift) * bq, bq)
        sc = _dot(k_ref[0, 0, pl.ds(koff, sub), :], qt_ref[0, :, pl.ds(qoff, bq)])
        s_scr[bank][j * sub:(j + 1) * sub, :] = sc
        return jnp.max(sc, axis=0, keepdims=True)

    def trip(n, bank, carry):
        m, cmax = carry
        chunk = n & (nkv - 1)
        m = jnp.where(chunk == 0, -jnp.inf, m)
        m_new = jnp.maximum(m, cmax)
        alpha = jnp.exp2(m - m_new)
        cmax_next = None
        pv = None
        for j in range(bkv // sub):
            cm = scores(n + 1, 1 - bank, j)
            cmax_next = cm if cmax_next is None else jnp.maximum(cmax_next, cm)
            p = jnp.exp2(s_scr[bank][j * sub:(j + 1) * sub, :] - m_new).astype(jnp.bfloat16)
            koff = pl.multiple_of(chunk * bkv + j * sub, sub)
            part = _dot(vt_ref[0, 0, :, pl.ds(koff, sub)], p)
            pv = part if pv is None else pv + part
        acc = alpha * acc_ref[...] + pv
        acc_ref[...] = acc

        @pl.when(chunk == nkv - 1)
        def _():
            qoff = pl.multiple_of((n >> kv_shift) * bq, bq)
            denom = acc[V_HEAD_DIM:V_HEAD_DIM + 1]
            gate = gt_ref[0, :, pl.ds(qoff, bq)].astype(jnp.float32)
            out = acc[:V_HEAD_DIM] / denom * gate
            o_ref[0, :, pl.ds(qoff, bq)] = out.astype(o_ref.dtype)

        return m_new, cmax_next

    acc_ref[...] = jnp.zeros(acc_ref.shape, acc_ref.dtype)
    cmax0 = scores(0, 0, 0)
    for j in range(1, bkv // sub):
        cmax0 = jnp.maximum(cmax0, scores(0, 0, j))
    carry = (jnp.zeros((1, bq), jnp.float32), cmax0)

    def steady(n, carry):
        return lax.cond((n & 1) == 1, lambda c: trip(n, 1, c), lambda c: trip(n, 0, c), carry)

    lax.fori_loop(0, ntrips, steady, carry)


def _attn_call(qt, k, vt, gt, *, bq, bkv, sub):
    b, _, s = qt.shape
    grid = (b, N_HEADS)
    return pl.pallas_call(
        functools.partial(_attn_kernel, bq=bq, bkv=bkv, sub=sub),
        grid=grid,
        in_specs=[
            pl.BlockSpec((1, HEAD_PAD, s), lambda bi, h: (bi, h, 0)),
            pl.BlockSpec((1, 1, s, HEAD_PAD), lambda bi, h: (bi, h, 0, 0)),
            pl.BlockSpec((1, 1, V_ROWS, s), lambda bi, h: (bi, h, 0, 0)),
            pl.BlockSpec((1, V_HEAD_DIM, s), lambda bi, h: (bi, h, 0)),
        ],
        out_specs=pl.BlockSpec((1, V_HEAD_DIM, s), lambda bi, h: (bi, h, 0)),
        out_shape=jax.ShapeDtypeStruct((b, ATTN_WIDTH, s), jnp.bfloat16),
        scratch_shapes=[pltpu.VMEM((bkv, bq), jnp.float32)] * 2
        + [pltpu.VMEM((V_ROWS, bq), jnp.float32)],
        compiler_params=pltpu.CompilerParams(
            dimension_semantics=("parallel", "parallel"),
            vmem_limit_bytes=VMEM_LIMIT_BYTES),
        name="attn",
    )(qt, k, vt, gt)


def _out_kernel(x_ref, cu_ref, cup_ref, cun_ref, g_ref, at_ref, convw_ref, wa_ref, wb_ref,
                npost_ref, y_ref):
    ti = pl.program_id(1)
    nt = pl.num_programs(1)
    cu = cu_ref[0].astype(jnp.float32)
    tm = cu.shape[0]
    prev_row = cup_ref[0, BF16_SUBLANES - 1:BF16_SUBLANES, :].astype(jnp.float32)
    next_row = cun_ref[0, 0:1, :].astype(jnp.float32)
    prev_row = jnp.where(ti > 0, prev_row, 0.0)
    next_row = jnp.where(ti < nt - 1, next_row, 0.0)
    row = lax.broadcasted_iota(jnp.int32, cu.shape, 0)
    up = jnp.where(row == 0, prev_row, pltpu.roll(cu, 1, 0))
    dn = jnp.where(row == tm - 1, next_row, pltpu.roll(cu, tm - 1, 0))
    w = convw_ref[...]
    conv = up * w[0:1] + cu * w[1:2] + dn * w[2:3]
    conv_out = (conv * g_ref[0].astype(jnp.float32)).astype(jnp.bfloat16)
    out = _dot(conv_out, wa_ref[...])
    out = out + lax.dot_general(at_ref[0], wb_ref[...], TN_DIMS,
                                preferred_element_type=jnp.float32)
    y_ref[0] = x_ref[0] + _rms(out, npost_ref[...])


def _out_call(x, cu, g, at, conv_w, wa, wb, npost, *, tm):
    b, s, d = x.shape
    grid = (b, s // tm)
    halo = BF16_SUBLANES
    nb = tm // halo
    last = s // halo - 1
    full = lambda arr: pl.BlockSpec(arr.shape, lambda bi, ti: (0,) * arr.ndim)
    return pl.pallas_call(
        _out_kernel,
        grid=grid,
        in_specs=[
            pl.BlockSpec((1, tm, d), lambda bi, ti: (bi, ti, 0)),
            pl.BlockSpec((1, tm, CONV_WIDTH), lambda bi, ti: (bi, ti, 0)),
            pl.BlockSpec((1, halo, CONV_WIDTH),
                         lambda bi, ti: (bi, jnp.maximum(ti * nb - 1, 0), 0)),
            pl.BlockSpec((1, halo, CONV_WIDTH),
                         lambda bi, ti: (bi, jnp.minimum((ti + 1) * nb, last), 0)),
            pl.BlockSpec((1, tm, CONV_WIDTH), lambda bi, ti: (bi, ti, 0)),
            pl.BlockSpec((1, ATTN_WIDTH, tm), lambda bi, ti: (bi, 0, ti)),
            full(conv_w), full(wa), full(wb), full(npost),
        ],
        out_specs=pl.BlockSpec((1, tm, d), lambda bi, ti: (bi, ti, 0)),
        out_shape=jax.ShapeDtypeStruct((b, s, d), x.dtype),
        compiler_params=pltpu.CompilerParams(
            dimension_semantics=("parallel", "parallel"),
            vmem_limit_bytes=VMEM_LIMIT_BYTES),
        name="outproj",
    )(x, cu, cu, cu, g, at, conv_w, wa, wb, npost)


def _prep_weights(norm_pre, w_in, q_norm, w_uq, kv_norm, w_ukv):
    bf16 = jnp.bfloat16
    wconv = w_in[:, :OFF_QLAT].astype(bf16)
    wlat = w_in[:, OFF_QLAT:OFF_KPE].astype(bf16)
    wzt = w_in[:, OFF_ZATTN:].T.astype(bf16)
    kpe = w_in[:, OFF_KPE:OFF_ZATTN]
    kpe_sw = jnp.concatenate([kpe[:, HALF_ROPE:], kpe[:, :HALF_ROPE]], axis=1)
    pad_l = jnp.zeros((D_MODEL, QK_NOPE_DIM), w_in.dtype)
    pad_r = jnp.zeros((D_MODEL, HEAD_PAD - QK_DIM), w_in.dtype)
    wkpe = jnp.concatenate([pad_l, kpe, pad_r, pad_l, kpe_sw, pad_r], axis=1).astype(bf16)
    wuq = w_uq.reshape(Q_LORA_RANK, N_HEADS, QK_DIM)
    wuq = jnp.pad(wuq, ((0, 0), (0, 0), (0, HEAD_PAD - QK_DIM)))
    wuqt = wuq.reshape(Q_LORA_RANK, N_HEADS * HEAD_PAD).T.astype(bf16)
    wukv = w_ukv.reshape(KV_LORA_RANK, N_HEADS, QK_NOPE_DIM + V_HEAD_DIM)
    wuk = jnp.pad(wukv[:, :, :QK_NOPE_DIM], ((0, 0), (0, 0), (0, HEAD_PAD - QK_NOPE_DIM)))
    wuk = wuk.reshape(KV_LORA_RANK, N_HEADS * HEAD_PAD).astype(bf16)
    wuvt = wukv[:, :, QK_NOPE_DIM:].reshape(KV_LORA_RANK, ATTN_WIDTH).T.astype(bf16)
    return (norm_pre.reshape(1, -1), wconv, wlat, wzt, wkpe, q_norm.reshape(1, -1), wuqt,
            kv_norm.reshape(1, -1), wuk, wuvt)


def _rope_tables(seq_len):
    freqs = 1.0 / (ROPE_THETA ** (jnp.arange(0, QK_ROPE_DIM, 2, dtype=jnp.float32) / QK_ROPE_DIM))
    ang = jnp.arange(seq_len, dtype=jnp.float32)[:, None] * freqs[None, :]
    cos, sin = jnp.cos(ang), jnp.sin(ang)
    zl = jnp.zeros((seq_len, QK_NOPE_DIM), jnp.float32)
    zr = jnp.zeros((seq_len, HEAD_PAD - QK_DIM), jnp.float32)
    cos2 = jnp.concatenate([zl, cos, cos, zr], axis=1)
    sin2 = jnp.concatenate([zl, -sin, sin, zr], axis=1)
    return cos2, sin2, cos.T, sin.T


def _layer(x, weights, conv_w, wa, wb, npost, *, tm_proj, bq, bkv, sub, tm_out):
    tables = _rope_tables(x.shape[1])
    cu, g, qt, k, vt, gt = _proj_call(x, weights, tables, tm=tm_proj)
    bkv = min(bkv, x.shape[1])
    at = _attn_call(qt, k, vt, gt, bq=bq, bkv=bkv, sub=min(sub, bkv))
    return _out_call(x, cu, g, at, conv_w, wa, wb, npost, tm=tm_out)


def kernel(x_prompt, x_sample, norm_pre, w_in, conv_w, q_norm, w_uq, kv_norm, w_ukv, w_out, norm_post):
    y_prompt, y_sample = x_prompt, x_sample
    for l in range(norm_pre.shape[0]):
        weights = _prep_weights(norm_pre[l], w_in[l], q_norm[l], w_uq[l], kv_norm[l], w_ukv[l])
        wa = w_out[l, :CONV_WIDTH].astype(jnp.bfloat16)
        wb = w_out[l, CONV_WIDTH:].astype(jnp.bfloat16)
        npost = norm_post[l].reshape(1, -1)
        cfg = dict(tm_proj=256, bq=256, bkv=8192, sub=8192, tm_out=512)
        y_prompt = _layer(y_prompt, weights, conv_w[l], wa, wb, npost, **cfg)
        y_sample = _layer(y_sample, weights, conv_w[l], wa, wb, npost, **cfg)
    return (y_prompt, y_sample)
```

```python
import functools
import math

import jax
import jax.numpy as jnp
from jax import lax
from jax.experimental import pallas as pl
from jax.experimental.pallas import tpu as pltpu

D_MODEL = 1024
CONV_WIDTH = 512
N_HEADS = 8
QK_NOPE_DIM = 64
QK_ROPE_DIM = 32
HALF_ROPE = QK_ROPE_DIM // 2
V_HEAD_DIM = 64
ATTN_WIDTH = N_HEADS * V_HEAD_DIM
Q_LORA_RANK = 384
KV_LORA_RANK = 256
ROPE_THETA = 10000.0
NORM_EPS = 1e-6
QK_DIM = QK_NOPE_DIM + QK_ROPE_DIM
HEAD_PAD = 128
V_ROWS = 80
LAT_WIDTH = Q_LORA_RANK + KV_LORA_RANK
OFF_QLAT = 4 * CONV_WIDTH
OFF_KVLAT = OFF_QLAT + Q_LORA_RANK
OFF_KPE = OFF_KVLAT + KV_LORA_RANK
OFF_ZATTN = OFF_KPE + QK_ROPE_DIM

BF16_SUBLANES = 16
VMEM_LIMIT_BYTES = 56 * 1024 * 1024

NT_DIMS = (((1,), (1,)), ((), ()))
TN_DIMS = (((0,), (0,)), ((), ()))


def _rms(x, g):
    return x * lax.rsqrt(jnp.mean(x * x, axis=-1, keepdims=True) + NORM_EPS) * g


def _silu(x):
    return x * (1.0 / (1.0 + jnp.exp(-x)))


def _dot(a, b):
    return jnp.dot(a, b, preferred_element_type=jnp.float32)


def _proj_kernel(x_ref, npre_ref, wconv_ref, wlat_ref, wzt_ref, qnorm_ref,
                 wuqt_ref, kvnorm_ref, wuk_ref, wuvt_ref, cos2_ref, sin2_ref, cost_ref,
                 sint_ref, cu_ref, g_ref, qt_ref, k_ref, vt_ref, gt_ref, *, q_scale):
    bf16 = jnp.bfloat16
    x = x_ref[0]
    tm = x.shape[0]
    hdn = _rms(x, npre_ref[...]).astype(bf16)

    pc = _dot(hdn, wconv_ref[...])
    cw = CONV_WIDTH
    cu_ref[0] = (pc[:, 2 * cw:3 * cw] * pc[:, 0:cw]).astype(bf16)
    g_ref[0] = (pc[:, cw:2 * cw] * _silu(pc[:, 3 * cw:4 * cw])).astype(bf16)

    lat = _dot(hdn, wlat_ref[...])
    qn = _rms(lat[:, :Q_LORA_RANK], qnorm_ref[...]).astype(bf16)
    kvn = _rms(lat[:, Q_LORA_RANK:LAT_WIDTH], kvnorm_ref[...]).astype(bf16)

    qt = lax.dot_general(wuqt_ref[...], qn, NT_DIMS, preferred_element_type=jnp.float32)
    qt = qt * q_scale
    qt_ref[0] = qt.astype(bf16)
    cos_t = cost_ref[...]
    sin_t = sint_ref[...]
    for h in range(N_HEADS):
        r1 = h * HEAD_PAD + QK_NOPE_DIM
        r2 = r1 + HALF_ROPE
        x1 = qt[r1:r2]
        x2 = qt[r2:r2 + HALF_ROPE]
        qt_ref[0, r1:r2, :] = (x1 * cos_t - x2 * sin_t).astype(bf16)
        qt_ref[0, r2:r2 + HALF_ROPE, :] = (x2 * cos_t + x1 * sin_t).astype(bf16)

    kn = _dot(kvn, wuk_ref[...])
    kp = lat[:, LAT_WIDTH:]
    roped = kp * cos2_ref[...] + pltpu.roll(kp, QK_ROPE_DIM, 1) * sin2_ref[...]
    for h in range(N_HEADS):
        k_ref[0, h] = (kn[:, h * HEAD_PAD:(h + 1) * HEAD_PAD] + roped).astype(bf16)

    vt = lax.dot_general(wuvt_ref[...], kvn, NT_DIMS, preferred_element_type=jnp.float32)
    row = lax.broadcasted_iota(jnp.int32, (V_ROWS - V_HEAD_DIM, tm), 0)
    ones_rows = jnp.where(row == 0, 1.0, 0.0).astype(bf16)
    for h in range(N_HEADS):
        vt_ref[0, h, 0:V_HEAD_DIM, :] = vt[h * V_HEAD_DIM:(h + 1) * V_HEAD_DIM].astype(bf16)
        vt_ref[0, h, V_HEAD_DIM:V_ROWS, :] = ones_rows

    zt = lax.dot_general(wzt_ref[...], hdn, NT_DIMS, preferred_element_type=jnp.float32)
    gt_ref[0] = _silu(zt).astype(bf16)


def _proj_call(x, weights, tables, *, tm):
    b, s, d = x.shape
    bf16 = jnp.bfloat16
    grid = (b, s // tm)
    full = lambda arr: pl.BlockSpec(arr.shape, lambda bi, ti: (0,) * arr.ndim)
    (npre, wconv, wlat, wzt, qnorm, wuqt, kvnorm, wuk, wuvt) = weights
    cos2, sin2, cos_t, sin_t = tables
    in_specs = [
        pl.BlockSpec((1, tm, d), lambda bi, ti: (bi, ti, 0)),
        full(npre), full(wconv), full(wlat), full(wzt), full(qnorm),
        full(wuqt), full(kvnorm), full(wuk), full(wuvt),
        pl.BlockSpec((tm, HEAD_PAD), lambda bi, ti: (ti, 0)),
        pl.BlockSpec((tm, HEAD_PAD), lambda bi, ti: (ti, 0)),
        pl.BlockSpec((HALF_ROPE, tm), lambda bi, ti: (0, ti)),
        pl.BlockSpec((HALF_ROPE, tm), lambda bi, ti: (0, ti)),
    ]
    out_shape = [
        jax.ShapeDtypeStruct((b, s, CONV_WIDTH), bf16),
        jax.ShapeDtypeStruct((b, s, CONV_WIDTH), bf16),
        jax.ShapeDtypeStruct((b, N_HEADS * HEAD_PAD, s), bf16),
        jax.ShapeDtypeStruct((b, N_HEADS, s, HEAD_PAD), bf16),
        jax.ShapeDtypeStruct((b, N_HEADS, V_ROWS, s), bf16),
        jax.ShapeDtypeStruct((b, ATTN_WIDTH, s), bf16),
    ]
    out_specs = [
        pl.BlockSpec((1, tm, CONV_WIDTH), lambda bi, ti: (bi, ti, 0)),
        pl.BlockSpec((1, tm, CONV_WIDTH), lambda bi, ti: (bi, ti, 0)),
        pl.BlockSpec((1, N_HEADS * HEAD_PAD, tm), lambda bi, ti: (bi, 0, ti)),
        pl.BlockSpec((1, N_HEADS, tm, HEAD_PAD), lambda bi, ti: (bi, 0, ti, 0)),
        pl.BlockSpec((1, N_HEADS, V_ROWS, tm), lambda bi, ti: (bi, 0, 0, ti)),
        pl.BlockSpec((1, ATTN_WIDTH, tm), lambda bi, ti: (bi, 0, ti)),
    ]
    q_scale = math.log2(math.e) / math.sqrt(QK_DIM)
    return pl.pallas_call(
        functools.partial(_proj_kernel, q_scale=q_scale),
        grid=grid, in_specs=in_specs, out_specs=out_specs, out_shape=out_shape,
        compiler_params=pltpu.CompilerParams(
            dimension_semantics=("parallel", "parallel"),
            vmem_limit_bytes=VMEM_LIMIT_BYTES),
        name="proj",
    )(x, npre, wconv, wlat, wzt, qnorm, wuqt, kvnorm, wuk, wuvt, cos2, sin2, cos_t, sin_t)


def _attn_kernel(qt_ref, k_ref, vt_ref, gt_ref, o_ref, s_a, s_b, acc_ref, *, bq, bkv, sub):
    s_scr = (s_a, s_b)
    s_len = k_ref.shape[2]
    nkv = s_len // bkv
    ntrips = (s_len // bq) * nkv
    assert nkv & (nkv - 1) == 0
    kv_shift = nkv.bit_length() - 1

    def scores(n, bank, j):
        n = jnp.minimum(n, ntrips - 1)
        koff = pl.multiple_of((n & (nkv - 1)) * bkv + j * sub, sub)
        qoff = pl.multiple_of((n >> kv_shift) * bq, bq)
        sc = _dot(k_ref[0, 0, pl.ds(koff, sub), :], qt_ref[0, :, pl.ds(qoff, bq)])
        s_scr[bank][j * sub:(j + 1) * sub, :] = sc
        return jnp.max(sc, axis=0, keepdims=True)

    def trip(n, bank, carry):
        m, cmax = carry
        chunk = n & (nkv - 1)
        m = jnp.where(chunk == 0, -jnp.inf, m)
        m_new = jnp.maximum(m, cmax)
        alpha = jnp.exp2(m - m_new)
        nsub = bkv // sub
        cmax_next = scores(n + 1, 1 - bank, 0)
        pv = None
        for j in range(nsub):
            if j + 1 < nsub:
                cmax_next = jnp.maximum(cmax_next, scores(n + 1, 1 - bank, j + 1))
            p = jnp.exp2(s_scr[bank][j * sub:(j + 1) * sub, :] - m_new).astype(jnp.bfloat16)
            koff = pl.multiple_of(chunk * bkv + j * sub, sub)
            part = _dot(vt_ref[0, 0, :, pl.ds(koff, sub)], p)
            pv = part if pv is None else pv + part
        acc = alpha * acc_ref[...] + pv
        acc_ref[...] = acc

        @pl.when(chunk == nkv - 1)
        def _():
            qoff = pl.multiple_of((n >> kv_shift) * bq, bq)
            denom = acc[V_HEAD_DIM:V_HEAD_DIM + 1]
            gate = gt_ref[0, :, pl.ds(qoff, bq)].astype(jnp.float32)
            out = acc[:V_HEAD_DIM] / denom * gate
            o_ref[0, :, pl.ds(qoff, bq)] = out.astype(o_ref.dtype)

        return m_new, cmax_next

    acc_ref[...] = jnp.zeros(acc_ref.shape, acc_ref.dtype)
    cmax0 = scores(0, 0, 0)
    for j in range(1, bkv // sub):
        cmax0 = jnp.maximum(cmax0, scores(0, 0, j))
    carry = (jnp.zeros((1, bq), jnp.float32), cmax0)

    def steady(n, carry):
        return lax.cond((n & 1) == 1, lambda c: trip(n, 1, c), lambda c: trip(n, 0, c), carry)

    lax.fori_loop(0, ntrips, steady, carry)


def _attn_call(qt, k, vt, gt, *, bq, bkv, sub, single_buffer):
    b, _, s = qt.shape
    grid = (b, N_HEADS)
    mode = dict(pipeline_mode=pl.Buffered(1)) if single_buffer else {}
    return pl.pallas_call(
        functools.partial(_attn_kernel, bq=bq, bkv=bkv, sub=sub),
        grid=grid,
        in_specs=[
            pl.BlockSpec((1, HEAD_PAD, s), lambda bi, h: (bi, h, 0), **mode),
            pl.BlockSpec((1, 1, s, HEAD_PAD), lambda bi, h: (bi, h, 0, 0), **mode),
            pl.BlockSpec((1, 1, V_ROWS, s), lambda bi, h: (bi, h, 0, 0), **mode),
            pl.BlockSpec((1, V_HEAD_DIM, s), lambda bi, h: (bi, h, 0), **mode),
        ],
        out_specs=pl.BlockSpec((1, V_HEAD_DIM, s), lambda bi, h: (bi, h, 0)),
        out_shape=jax.ShapeDtypeStruct((b, ATTN_WIDTH, s), jnp.bfloat16),
        scratch_shapes=[pltpu.VMEM((bkv, bq), jnp.float32)] * 2
        + [pltpu.VMEM((V_ROWS, bq), jnp.float32)],
        compiler_params=pltpu.CompilerParams(
            dimension_semantics=("parallel", "parallel"),
            vmem_limit_bytes=VMEM_LIMIT_BYTES),
        name="attn",
    )(qt, k, vt, gt)


def _out_kernel(x_ref, cu_ref, cup_ref, cun_ref, g_ref, at_ref, convw_ref, wa_ref, wb_ref,
                npost_ref, y_ref):
    ti = pl.program_id(1)
    nt = pl.num_programs(1)
    cu = cu_ref[0].astype(jnp.float32)
    tm = cu.shape[0]
    prev_row = cup_ref[0, BF16_SUBLANES - 1:BF16_SUBLANES, :].astype(jnp.float32)
    next_row = cun_ref[0, 0:1, :].astype(jnp.float32)
    prev_row = jnp.where(ti > 0, prev_row, 0.0)
    next_row = jnp.where(ti < nt - 1, next_row, 0.0)
    row = lax.broadcasted_iota(jnp.int32, cu.shape, 0)
    up = jnp.where(row == 0, prev_row, pltpu.roll(cu, 1, 0))
    dn = jnp.where(row == tm - 1, next_row, pltpu.roll(cu, tm - 1, 0))
    w = convw_ref[...]
    conv = up * w[0:1] + cu * w[1:2] + dn * w[2:3]
    conv_out = (conv * g_ref[0].astype(jnp.float32)).astype(jnp.bfloat16)
    out = _dot(conv_out, wa_ref[...])
    out = out + lax.dot_general(at_ref[0], wb_ref[...], TN_DIMS,
                                preferred_element_type=jnp.float32)
    y_ref[0] = x_ref[0] + _rms(out, npost_ref[...])


def _out_call(x, cu, g, at, conv_w, wa, wb, npost, *, tm):
    b, s, d = x.shape
    grid = (b, s // tm)
    halo = BF16_SUBLANES
    nb = tm // halo
    last = s // halo - 1
    full = lambda arr: pl.BlockSpec(arr.shape, lambda bi, ti: (0,) * arr.ndim)
    return pl.pallas_call(
        _out_kernel,
        grid=grid,
        in_specs=[
            pl.BlockSpec((1, tm, d), lambda bi, ti: (bi, ti, 0)),
            pl.BlockSpec((1, tm, CONV_WIDTH), lambda bi, ti: (bi, ti, 0)),
            pl.BlockSpec((1, halo, CONV_WIDTH),
                         lambda bi, ti: (bi, jnp.maximum(ti * nb - 1, 0), 0)),
            pl.BlockSpec((1, halo, CONV_WIDTH),
                         lambda bi, ti: (bi, jnp.minimum((ti + 1) * nb, last), 0)),
            pl.BlockSpec((1, tm, CONV_WIDTH), lambda bi, ti: (bi, ti, 0)),
            pl.BlockSpec((1, ATTN_WIDTH, tm), lambda bi, ti: (bi, 0, ti)),
            full(conv_w), full(wa), full(wb), full(npost),
        ],
        out_specs=pl.BlockSpec((1, tm, d), lambda bi, ti: (bi, ti, 0)),
        out_shape=jax.ShapeDtypeStruct((b, s, d), x.dtype),
        compiler_params=pltpu.CompilerParams(
            dimension_semantics=("parallel", "parallel"),
            vmem_limit_bytes=VMEM_LIMIT_BYTES),
        name="outproj",
    )(x, cu, cu, cu, g, at, conv_w, wa, wb, npost)


def _prep_weights(norm_pre, w_in, q_norm, w_uq, kv_norm, w_ukv):
    bf16 = jnp.bfloat16
    wconv = w_in[:, :OFF_QLAT].astype(bf16)
    wzt = w_in[:, OFF_ZATTN:].T.astype(bf16)
    kpe = w_in[:, OFF_KPE:OFF_ZATTN]
    kpe_sw = jnp.concatenate([kpe[:, HALF_ROPE:], kpe[:, :HALF_ROPE]], axis=1)
    pad = jnp.zeros((D_MODEL, QK_ROPE_DIM), w_in.dtype)
    wlat = jnp.concatenate([w_in[:, OFF_QLAT:OFF_KPE], pad, kpe_sw, kpe, pad], axis=1).astype(bf16)
    wuq = w_uq.reshape(Q_LORA_RANK, N_HEADS, QK_DIM)
    wuq = jnp.pad(wuq, ((0, 0), (0, 0), (0, HEAD_PAD - QK_DIM)))
    wuqt = wuq.reshape(Q_LORA_RANK, N_HEADS * HEAD_PAD).T.astype(bf16)
    wukv = w_ukv.reshape(KV_LORA_RANK, N_HEADS, QK_NOPE_DIM + V_HEAD_DIM)
    wuk = jnp.pad(wukv[:, :, :QK_NOPE_DIM], ((0, 0), (0, 0), (0, HEAD_PAD - QK_NOPE_DIM)))
    wuk = wuk.reshape(KV_LORA_RANK, N_HEADS * HEAD_PAD).astype(bf16)
    wuvt = wukv[:, :, QK_NOPE_DIM:].reshape(KV_LORA_RANK, ATTN_WIDTH).T.astype(bf16)
    return (norm_pre.reshape(1, -1), wconv, wlat, wzt, q_norm.reshape(1, -1), wuqt,
            kv_norm.reshape(1, -1), wuk, wuvt)


def _rope_tables(seq_len):
    freqs = 1.0 / (ROPE_THETA ** (jnp.arange(0, QK_ROPE_DIM, 2, dtype=jnp.float32) / QK_ROPE_DIM))
    ang = jnp.arange(seq_len, dtype=jnp.float32)[:, None] * freqs[None, :]
    cos, sin = jnp.cos(ang), jnp.sin(ang)
    zl = jnp.zeros((seq_len, QK_NOPE_DIM), jnp.float32)
    zr = jnp.zeros((seq_len, HEAD_PAD - QK_DIM), jnp.float32)
    cos2 = jnp.concatenate([zl, cos, cos, zr], axis=1)
    sin2 = jnp.concatenate([zl, -sin, sin, zr], axis=1)
    return cos2, sin2, cos.T, sin.T


def _layer(x, weights, conv_w, wa, wb, npost, *, tm_proj, bq, bkv, sub, tm_out):
    tables = _rope_tables(x.shape[1])
    cu, g, qt, k, vt, gt = _proj_call(x, weights, tables, tm=tm_proj)
    bkv = min(bkv, x.shape[1])
    at = _attn_call(qt, k, vt, gt, bq=bq, bkv=bkv, sub=min(sub, bkv),
                    single_buffer=bkv > 8192)
    return _out_call(x, cu, g, at, conv_w, wa, wb, npost, tm=tm_out)


def kernel(x_prompt, x_sample, norm_pre, w_in, conv_w, q_norm, w_uq, kv_norm, w_ukv, w_out, norm_post):
    y_prompt, y_sample = x_prompt, x_sample
    for l in range(norm_pre.shape[0]):
        weights = _prep_weights(norm_pre[l], w_in[l], q_norm[l], w_uq[l], kv_norm[l], w_ukv[l])
        wa = w_out[l, :CONV_WIDTH].astype(jnp.bfloat16)
        wb = w_out[l, CONV_WIDTH:].astype(jnp.bfloat16)
        npost = norm_post[l].reshape(1, -1)
        cfg = dict(tm_proj=512, bq=256, bkv=8192, sub=8192, tm_out=512)
        y_prompt = _layer(y_prompt, weights, conv_w[l], wa, wb, npost, **cfg)
        y_sample = _layer(y_sample, weights, conv_w[l], wa, wb, npost, **cfg)
    return (y_prompt, y_sample)
```

```python
import functools
import math

import jax
import jax.numpy as jnp
from jax import lax
from jax.experimental import pallas as pl
from jax.experimental.pallas import tpu as pltpu

D_MODEL = 1024
CONV_WIDTH = 512
N_HEADS = 8
QK_NOPE_DIM = 64
QK_ROPE_DIM = 32
HALF_ROPE = QK_ROPE_DIM // 2
V_HEAD_DIM = 64
ATTN_WIDTH = N_HEADS * V_HEAD_DIM
Q_LORA_RANK = 384
KV_LORA_RANK = 256
ROPE_THETA = 10000.0
NORM_EPS = 1e-6
QK_DIM = QK_NOPE_DIM + QK_ROPE_DIM
HEAD_PAD = 128
V_ROWS = 80
LAT_WIDTH = Q_LORA_RANK + KV_LORA_RANK
OFF_QLAT = 4 * CONV_WIDTH
OFF_KVLAT = OFF_QLAT + Q_LORA_RANK
OFF_KPE = OFF_KVLAT + KV_LORA_RANK
OFF_ZATTN = OFF_KPE + QK_ROPE_DIM

BF16_SUBLANES = 16
VMEM_LIMIT_BYTES = 56 * 1024 * 1024
SCORE_BANKS_DOUBLE_BUFFER_LIMIT = 16 * 1024 * 1024

NT_DIMS = (((1,), (1,)), ((), ()))
TN_DIMS = (((0,), (0,)), ((), ()))


def _rms(x, g):
    return x * lax.rsqrt(jnp.mean(x * x, axis=-1, keepdims=True) + NORM_EPS) * g


def _silu(x):
    return x * (1.0 / (1.0 + jnp.exp(-x)))


def _dot(a, b):
    return jnp.dot(a, b, preferred_element_type=jnp.float32)


def _proj_kernel(x_ref, npre_ref, wconv_ref, wlat_ref, wzt_ref, qnorm_ref,
                 wuqt_ref, kvnorm_ref, wuk_ref, wuvt_ref, cos2_ref, sin2_ref, cost_ref,
                 sint_ref, cu_ref, g_ref, qt_ref, k_ref, vt_ref, gt_ref, *, q_scale):
    bf16 = jnp.bfloat16
    x = x_ref[0]
    tm = x.shape[0]
    hdn = _rms(x, npre_ref[...]).astype(bf16)

    pc = _dot(hdn, wconv_ref[...])
    cw = CONV_WIDTH
    cu_ref[0] = (pc[:, 2 * cw:3 * cw] * pc[:, 0:cw]).astype(bf16)
    g_ref[0] = (pc[:, cw:2 * cw] * _silu(pc[:, 3 * cw:4 * cw])).astype(bf16)

    lat = _dot(hdn, wlat_ref[...])
    qn = _rms(lat[:, :Q_LORA_RANK], qnorm_ref[...]).astype(bf16)
    kvn = _rms(lat[:, Q_LORA_RANK:LAT_WIDTH], kvnorm_ref[...]).astype(bf16)

    qt = lax.dot_general(wuqt_ref[...], qn, NT_DIMS, preferred_element_type=jnp.float32)
    qt = qt * q_scale
    qt_ref[0] = qt.astype(bf16)
    cos_t = cost_ref[...]
    sin_t = sint_ref[...]
    for h in range(N_HEADS):
        r1 = h * HEAD_PAD + QK_NOPE_DIM
        r2 = r1 + HALF_ROPE
        x1 = qt[r1:r2]
        x2 = qt[r2:r2 + HALF_ROPE]
        qt_ref[0, r1:r2, :] = (x1 * cos_t - x2 * sin_t).astype(bf16)
        qt_ref[0, r2:r2 + HALF_ROPE, :] = (x2 * cos_t + x1 * sin_t).astype(bf16)

    kn = _dot(kvn, wuk_ref[...])
    kp = lat[:, LAT_WIDTH:]
    roped = kp * cos2_ref[...] + pltpu.roll(kp, QK_ROPE_DIM, 1) * sin2_ref[...]
    for h in range(N_HEADS):
        k_ref[0, h] = (kn[:, h * HEAD_PAD:(h + 1) * HEAD_PAD] + roped).astype(bf16)

    vt = lax.dot_general(wuvt_ref[...], kvn, NT_DIMS, preferred_element_type=jnp.float32)
    row = lax.broadcasted_iota(jnp.int32, (V_ROWS - V_HEAD_DIM, tm), 0)
    ones_rows = jnp.where(row == 0, 1.0, 0.0).astype(bf16)
    for h in range(N_HEADS):
        vt_ref[0, h, 0:V_HEAD_DIM, :] = vt[h * V_HEAD_DIM:(h + 1) * V_HEAD_DIM].astype(bf16)
        vt_ref[0, h, V_HEAD_DIM:V_ROWS, :] = ones_rows

    zt = lax.dot_general(wzt_ref[...], hdn, NT_DIMS, preferred_element_type=jnp.float32)
    gt_ref[0] = _silu(zt).astype(bf16)


def _proj_call(x, weights, tables, *, tm):
    b, s, d = x.shape
    bf16 = jnp.bfloat16
    grid = (b, s // tm)
    full = lambda arr: pl.BlockSpec(arr.shape, lambda bi, ti: (0,) * arr.ndim)
    (npre, wconv, wlat, wzt, qnorm, wuqt, kvnorm, wuk, wuvt) = weights
    cos2, sin2, cos_t, sin_t = tables
    in_specs = [
        pl.BlockSpec((1, tm, d), lambda bi, ti: (bi, ti, 0)),
        full(npre), full(wconv), full(wlat), full(wzt), full(qnorm),
        full(wuqt), full(kvnorm), full(wuk), full(wuvt),
        pl.BlockSpec((tm, HEAD_PAD), lambda bi, ti: (ti, 0)),
        pl.BlockSpec((tm, HEAD_PAD), lambda bi, ti: (ti, 0)),
        pl.BlockSpec((HALF_ROPE, tm), lambda bi, ti: (0, ti)),
        pl.BlockSpec((HALF_ROPE, tm), lambda bi, ti: (0, ti)),
    ]
    out_shape = [
        jax.ShapeDtypeStruct((b, s, CONV_WIDTH), bf16),
        jax.ShapeDtypeStruct((b, s, CONV_WIDTH), bf16),
        jax.ShapeDtypeStruct((b, N_HEADS * HEAD_PAD, s), bf16),
        jax.ShapeDtypeStruct((b, N_HEADS, s, HEAD_PAD), bf16),
        jax.ShapeDtypeStruct((b, N_HEADS, V_ROWS, s), bf16),
        jax.ShapeDtypeStruct((b, ATTN_WIDTH, s), bf16),
    ]
    out_specs = [
        pl.BlockSpec((1, tm, CONV_WIDTH), lambda bi, ti: (bi, ti, 0)),
        pl.BlockSpec((1, tm, CONV_WIDTH), lambda bi, ti: (bi, ti, 0)),
        pl.BlockSpec((1, N_HEADS * HEAD_PAD, tm), lambda bi, ti: (bi, 0, ti)),
        pl.BlockSpec((1, N_HEADS, tm, HEAD_PAD), lambda bi, ti: (bi, 0, ti, 0)),
        pl.BlockSpec((1, N_HEADS, V_ROWS, tm), lambda bi, ti: (bi, 0, 0, ti)),
        pl.BlockSpec((1, ATTN_WIDTH, tm), lambda bi, ti: (bi, 0, ti)),
    ]
    q_scale = math.log2(math.e) / math.sqrt(QK_DIM)
    return pl.pallas_call(
        functools.partial(_proj_kernel, q_scale=q_scale),
        grid=grid, in_specs=in_specs, out_specs=out_specs, out_shape=out_shape,
        compiler_params=pltpu.CompilerParams(
            dimension_semantics=("parallel", "parallel"),
            vmem_limit_bytes=VMEM_LIMIT_BYTES),
        name="proj",
    )(x, npre, wconv, wlat, wzt, qnorm, wuqt, kvnorm, wuk, wuvt, cos2, sin2, cos_t, sin_t)


def _attn_kernel(qt_ref, k_ref, vt_ref, gt_ref, o_ref, s_a, s_b, acc_ref, *, bq, bkv, sub):
    s_scr = (s_a, s_b)
    s_len = k_ref.shape[2]
    nkv = s_len // bkv
    ntrips = (s_len // bq) * nkv
    assert nkv & (nkv - 1) == 0
    kv_shift = nkv.bit_length() - 1

    def scores(n, bank, j):
        n = jnp.minimum(n, ntrips - 1)
        koff = pl.multiple_of((n & (nkv - 1)) * bkv + j * sub, sub)
        qoff = pl.multiple_of((n >> kv_shift) * bq, bq)
        sc = _dot(k_ref[0, 0, pl.ds(koff, sub), :], qt_ref[0, :, pl.ds(qoff, bq)])
        s_scr[bank][j * sub:(j + 1) * sub, :] = sc
        return jnp.max(sc, axis=0, keepdims=True)

    def trip(n, bank, carry):
        m, cmax = carry
        chunk = n & (nkv - 1)
        m = jnp.where(chunk == 0, -jnp.inf, m)
        m_new = jnp.maximum(m, cmax)
        alpha = jnp.exp2(m - m_new)
        nsub = bkv // sub
        cmax_next = scores(n + 1, 1 - bank, 0)
        pv = None
        for j in range(nsub):
            if j + 1 < nsub:
                cmax_next = jnp.maximum(cmax_next, scores(n + 1, 1 - bank, j + 1))
            p = jnp.exp2(s_scr[bank][j * sub:(j + 1) * sub, :] - m_new).astype(jnp.bfloat16)
            koff = pl.multiple_of(chunk * bkv + j * sub, sub)
            part = _dot(vt_ref[0, 0, :, pl.ds(koff, sub)], p)
            pv = part if pv is None else pv + part
        acc = alpha * acc_ref[...] + pv
        acc_ref[...] = acc

        @pl.when(chunk == nkv - 1)
        def _():
            qoff = pl.multiple_of((n >> kv_shift) * bq, bq)
            denom = acc[V_HEAD_DIM:V_HEAD_DIM + 1]
            gate = gt_ref[0, :, pl.ds(qoff, bq)].astype(jnp.float32)
            out = acc[:V_HEAD_DIM] / denom * gate
            o_ref[0, :, pl.ds(qoff, bq)] = out.astype(o_ref.dtype)

        return m_new, cmax_next

    acc_ref[...] = jnp.zeros(acc_ref.shape, acc_ref.dtype)
    cmax0 = scores(0, 0, 0)
    for j in range(1, bkv // sub):
        cmax0 = jnp.maximum(cmax0, scores(0, 0, j))
    carry = (jnp.zeros((1, bq), jnp.float32), cmax0)

    def steady(n, carry):
        return lax.cond((n & 1) == 1, lambda c: trip(n, 1, c), lambda c: trip(n, 0, c), carry)

    lax.fori_loop(0, ntrips, steady, carry)


def _attn_call(qt, k, vt, gt, *, bq, bkv, sub, single_buffer):
    b, _, s = qt.shape
    grid = (b, N_HEADS)
    mode = dict(pipeline_mode=pl.Buffered(1)) if single_buffer else {}
    return pl.pallas_call(
        functools.partial(_attn_kernel, bq=bq, bkv=bkv, sub=sub),
        grid=grid,
        in_specs=[
            pl.BlockSpec((1, HEAD_PAD, s), lambda bi, h: (bi, h, 0), **mode),
            pl.BlockSpec((1, 1, s, HEAD_PAD), lambda bi, h: (bi, h, 0, 0), **mode),
            pl.BlockSpec((1, 1, V_ROWS, s), lambda bi, h: (bi, h, 0, 0), **mode),
            pl.BlockSpec((1, V_HEAD_DIM, s), lambda bi, h: (bi, h, 0), **mode),
        ],
        out_specs=pl.BlockSpec((1, V_HEAD_DIM, s), lambda bi, h: (bi, h, 0)),
        out_shape=jax.ShapeDtypeStruct((b, ATTN_WIDTH, s), jnp.bfloat16),
        scratch_shapes=[pltpu.VMEM((bkv, bq), jnp.float32)] * 2
        + [pltpu.VMEM((V_ROWS, bq), jnp.float32)],
        compiler_params=pltpu.CompilerParams(
            dimension_semantics=("parallel", "parallel"),
            vmem_limit_bytes=VMEM_LIMIT_BYTES),
        name="attn",
    )(qt, k, vt, gt)


def _out_kernel(x_ref, cu_ref, cup_ref, cun_ref, g_ref, at_ref, convw_ref, wa_ref, wb_ref,
                npost_ref, y_ref):
    ti = pl.program_id(1)
    nt = pl.num_programs(1)
    cu = cu_ref[0].astype(jnp.float32)
    tm = cu.shape[0]
    prev_row = cup_ref[0, BF16_SUBLANES - 1:BF16_SUBLANES, :].astype(jnp.float32)
    next_row = cun_ref[0, 0:1, :].astype(jnp.float32)
    prev_row = jnp.where(ti > 0, prev_row, 0.0)
    next_row = jnp.where(ti < nt - 1, next_row, 0.0)
    row = lax.broadcasted_iota(jnp.int32, cu.shape, 0)
    up = jnp.where(row == 0, prev_row, pltpu.roll(cu, 1, 0))
    dn = jnp.where(row == tm - 1, next_row, pltpu.roll(cu, tm - 1, 0))
    w = convw_ref[...]
    conv = up * w[0:1] + cu * w[1:2] + dn * w[2:3]
    conv_out = (conv * g_ref[0].astype(jnp.float32)).astype(jnp.bfloat16)
    out = _dot(conv_out, wa_ref[...])
    out = out + lax.dot_general(at_ref[0], wb_ref[...], TN_DIMS,
                                preferred_element_type=jnp.float32)
    y_ref[0] = x_ref[0] + _rms(out, npost_ref[...])


def _out_call(x, cu, g, at, conv_w, wa, wb, npost, *, tm):
    b, s, d = x.shape
    grid = (b, s // tm)
    halo = BF16_SUBLANES
    nb = tm // halo
    last = s // halo - 1
    full = lambda arr: pl.BlockSpec(arr.shape, lambda bi, ti: (0,) * arr.ndim)
    return pl.pallas_call(
        _out_kernel,
        grid=grid,
        in_specs=[
            pl.BlockSpec((1, tm, d), lambda bi, ti: (bi, ti, 0)),
            pl.BlockSpec((1, tm, CONV_WIDTH), lambda bi, ti: (bi, ti, 0)),
            pl.BlockSpec((1, halo, CONV_WIDTH),
                         lambda bi, ti: (bi, jnp.maximum(ti * nb - 1, 0), 0)),
            pl.BlockSpec((1, halo, CONV_WIDTH),
                         lambda bi, ti: (bi, jnp.minimum((ti + 1) * nb, last), 0)),
            pl.BlockSpec((1, tm, CONV_WIDTH), lambda bi, ti: (bi, ti, 0)),
            pl.BlockSpec((1, ATTN_WIDTH, tm), lambda bi, ti: (bi, 0, ti)),
            full(conv_w), full(wa), full(wb), full(npost),
        ],
        out_specs=pl.BlockSpec((1, tm, d), lambda bi, ti: (bi, ti, 0)),
        out_shape=jax.ShapeDtypeStruct((b, s, d), x.dtype),
        compiler_params=pltpu.CompilerParams(
            dimension_semantics=("parallel", "parallel"),
            vmem_limit_bytes=VMEM_LIMIT_BYTES),
        name="outproj",
    )(x, cu, cu, cu, g, at, conv_w, wa, wb, npost)


def _prep_weights(norm_pre, w_in, q_norm, w_uq, kv_norm, w_ukv):
    bf16 = jnp.bfloat16
    wconv = w_in[:, :OFF_QLAT].astype(bf16)
    wzt = w_in[:, OFF_ZATTN:].T.astype(bf16)
    kpe = w_in[:, OFF_KPE:OFF_ZATTN]
    kpe_sw = jnp.concatenate([kpe[:, HALF_ROPE:], kpe[:, :HALF_ROPE]], axis=1)
    pad = jnp.zeros((D_MODEL, QK_ROPE_DIM), w_in.dtype)
    wlat = jnp.concatenate([w_in[:, OFF_QLAT:OFF_KPE], pad, kpe_sw, kpe, pad], axis=1).astype(bf16)
    wuq = w_uq.reshape(Q_LORA_RANK, N_HEADS, QK_DIM)
    wuq = jnp.pad(wuq, ((0, 0), (0, 0), (0, HEAD_PAD - QK_DIM)))
    wuqt = wuq.reshape(Q_LORA_RANK, N_HEADS * HEAD_PAD).T.astype(bf16)
    wukv = w_ukv.reshape(KV_LORA_RANK, N_HEADS, QK_NOPE_DIM + V_HEAD_DIM)
    wuk = jnp.pad(wukv[:, :, :QK_NOPE_DIM], ((0, 0), (0, 0), (0, HEAD_PAD - QK_NOPE_DIM)))
    wuk = wuk.reshape(KV_LORA_RANK, N_HEADS * HEAD_PAD).astype(bf16)
    wuvt = wukv[:, :, QK_NOPE_DIM:].reshape(KV_LORA_RANK, ATTN_WIDTH).T.astype(bf16)
    return (norm_pre.reshape(1, -1), wconv, wlat, wzt, q_norm.reshape(1, -1), wuqt,
            kv_norm.reshape(1, -1), wuk, wuvt)


def _rope_tables(seq_len):
    freqs = 1.0 / (ROPE_THETA ** (jnp.arange(0, QK_ROPE_DIM, 2, dtype=jnp.float32) / QK_ROPE_DIM))
    ang = jnp.arange(seq_len, dtype=jnp.float32)[:, None] * freqs[None, :]
    cos, sin = jnp.cos(ang), jnp.sin(ang)
    zl = jnp.zeros((seq_len, QK_NOPE_DIM), jnp.float32)
    zr = jnp.zeros((seq_len, HEAD_PAD - QK_DIM), jnp.float32)
    cos2 = jnp.concatenate([zl, cos, cos, zr], axis=1)
    sin2 = jnp.concatenate([zl, -sin, sin, zr], axis=1)
    return cos2, sin2, cos.T, sin.T


def _layer(x, weights, conv_w, wa, wb, npost, *, tm_proj, bq, bkv, sub, tm_out):
    tables = _rope_tables(x.shape[1])
    cu, g, qt, k, vt, gt = _proj_call(x, weights, tables, tm=tm_proj)
    bkv = min(bkv, x.shape[1])
    at = _attn_call(qt, k, vt, gt, bq=bq, bkv=bkv, sub=min(sub, bkv),
                    single_buffer=2 * bkv * bq * 4 > SCORE_BANKS_DOUBLE_BUFFER_LIMIT)
    return _out_call(x, cu, g, at, conv_w, wa, wb, npost, tm=tm_out)


def kernel(x_prompt, x_sample, norm_pre, w_in, conv_w, q_norm, w_uq, kv_norm, w_ukv, w_out, norm_post):
    y_prompt, y_sample = x_prompt, x_sample
    for l in range(norm_pre.shape[0]):
        weights = _prep_weights(norm_pre[l], w_in[l], q_norm[l], w_uq[l], kv_norm[l], w_ukv[l])
        wa = w_out[l, :CONV_WIDTH].astype(jnp.bfloat16)
        wb = w_out[l, CONV_WIDTH:].astype(jnp.bfloat16)
        npost = norm_post[l].reshape(1, -1)
        cfg = dict(tm_proj=512, bq=512, bkv=8192, sub=256, tm_out=512)
        y_prompt = _layer(y_prompt, weights, conv_w[l], wa, wb, npost, **cfg)
        y_sample = _layer(y_sample, weights, conv_w[l], wa, wb, npost, **cfg)
    return (y_prompt, y_sample)
```

```python
import functools
import math

import jax
import jax.numpy as jnp
from jax import lax
from jax.experimental import pallas as pl
from jax.experimental.pallas import tpu as pltpu

D_MODEL = 1024
CONV_WIDTH = 512
N_HEADS = 8
QK_NOPE_DIM = 64
QK_ROPE_DIM = 32
HALF_ROPE = QK_ROPE_DIM // 2
V_HEAD_DIM = 64
ATTN_WIDTH = N_HEADS * V_HEAD_DIM
Q_LORA_RANK = 384
KV_LORA_RANK = 256
ROPE_THETA = 10000.0
NORM_EPS = 1e-6
QK_DIM = QK_NOPE_DIM + QK_ROPE_DIM
HEAD_PAD = 128
V_ROWS = 80
LAT_WIDTH = Q_LORA_RANK + KV_LORA_RANK
OFF_QLAT = 4 * CONV_WIDTH
OFF_KVLAT = OFF_QLAT + Q_LORA_RANK
OFF_KPE = OFF_KVLAT + KV_LORA_RANK
OFF_ZATTN = OFF_KPE + QK_ROPE_DIM

BF16_SUBLANES = 16
VMEM_LIMIT_BYTES = 56 * 1024 * 1024
SCORE_BANKS_DOUBLE_BUFFER_LIMIT = 16 * 1024 * 1024
ATTN_WINDOW_ROWS = 16384

NT_DIMS = (((1,), (1,)), ((), ()))
TN_DIMS = (((0,), (0,)), ((), ()))


def _rms(x, g):
    return x * lax.rsqrt(jnp.mean(x * x, axis=-1, keepdims=True) + NORM_EPS) * g


def _silu(x):
    return x * (1.0 / (1.0 + jnp.exp(-x)))


def _dot(a, b):
    return jnp.dot(a, b, preferred_element_type=jnp.float32)


def _proj_kernel(x_ref, npre_ref, wconv_ref, wlat_ref, wzt_ref, qnorm_ref,
                 wuqt_ref, kvnorm_ref, wuk_ref, wuvt_ref, cos2_ref, sin2_ref, cost_ref,
                 sint_ref, cu_ref, g_ref, qt_ref, k_ref, vt_ref, gt_ref, *, q_scale):
    bf16 = jnp.bfloat16
    x = x_ref[0]
    tm = x.shape[0]
    hdn = _rms(x, npre_ref[...]).astype(bf16)

    pc = _dot(hdn, wconv_ref[...])
    cw = CONV_WIDTH
    cu_ref[0] = (pc[:, 2 * cw:3 * cw] * pc[:, 0:cw]).astype(bf16)
    g_ref[0] = (pc[:, cw:2 * cw] * _silu(pc[:, 3 * cw:4 * cw])).astype(bf16)

    lat = _dot(hdn, wlat_ref[...])
    qn = _rms(lat[:, :Q_LORA_RANK], qnorm_ref[...]).astype(bf16)
    kvn = _rms(lat[:, Q_LORA_RANK:LAT_WIDTH], kvnorm_ref[...]).astype(bf16)

    qt = lax.dot_general(wuqt_ref[...], qn, NT_DIMS, preferred_element_type=jnp.float32)
    qt = qt * q_scale
    qt_ref[0] = qt.astype(bf16)
    cos_t = cost_ref[...]
    sin_t = sint_ref[...]
    for h in range(N_HEADS):
        r1 = h * HEAD_PAD + QK_NOPE_DIM
        r2 = r1 + HALF_ROPE
        x1 = qt[r1:r2]
        x2 = qt[r2:r2 + HALF_ROPE]
        qt_ref[0, r1:r2, :] = (x1 * cos_t - x2 * sin_t).astype(bf16)
        qt_ref[0, r2:r2 + HALF_ROPE, :] = (x2 * cos_t + x1 * sin_t).astype(bf16)

    kn = _dot(kvn, wuk_ref[...])
    kp = lat[:, LAT_WIDTH:]
    roped = kp * cos2_ref[...] + pltpu.roll(kp, QK_ROPE_DIM, 1) * sin2_ref[...]
    for h in range(N_HEADS):
        k_ref[0, h] = (kn[:, h * HEAD_PAD:(h + 1) * HEAD_PAD] + roped).astype(bf16)

    vt = lax.dot_general(wuvt_ref[...], kvn, NT_DIMS, preferred_element_type=jnp.float32)
    row = lax.broadcasted_iota(jnp.int32, (V_ROWS - V_HEAD_DIM, tm), 0)
    ones_rows = jnp.where(row == 0, 1.0, 0.0).astype(bf16)
    for h in range(N_HEADS):
        vt_ref[0, h, 0:V_HEAD_DIM, :] = vt[h * V_HEAD_DIM:(h + 1) * V_HEAD_DIM].astype(bf16)
        vt_ref[0, h, V_HEAD_DIM:V_ROWS, :] = ones_rows

    zt = lax.dot_general(wzt_ref[...], hdn, NT_DIMS, preferred_element_type=jnp.float32)
    gt_ref[0] = _silu(zt).astype(bf16)


def _proj_call(x, weights, tables, *, tm):
    b, s, d = x.shape
    bf16 = jnp.bfloat16
    grid = (b, s // tm)
    full = lambda arr: pl.BlockSpec(arr.shape, lambda bi, ti: (0,) * arr.ndim)
    (npre, wconv, wlat, wzt, qnorm, wuqt, kvnorm, wuk, wuvt) = weights
    cos2, sin2, cos_t, sin_t = tables
    in_specs = [
        pl.BlockSpec((1, tm, d), lambda bi, ti: (bi, ti, 0)),
        full(npre), full(wconv), full(wlat), full(wzt), full(qnorm),
        full(wuqt), full(kvnorm), full(wuk), full(wuvt),
        pl.BlockSpec((tm, HEAD_PAD), lambda bi, ti: (ti, 0)),
        pl.BlockSpec((tm, HEAD_PAD), lambda bi, ti: (ti, 0)),
        pl.BlockSpec((HALF_ROPE, tm), lambda bi, ti: (0, ti)),
        pl.BlockSpec((HALF_ROPE, tm), lambda bi, ti: (0, ti)),
    ]
    out_shape = [
        jax.ShapeDtypeStruct((b, s, CONV_WIDTH), bf16),
        jax.ShapeDtypeStruct((b, s, CONV_WIDTH), bf16),
        jax.ShapeDtypeStruct((b, N_HEADS * HEAD_PAD, s), bf16),
        jax.ShapeDtypeStruct((b, N_HEADS, s, HEAD_PAD), bf16),
        jax.ShapeDtypeStruct((b, N_HEADS, V_ROWS, s), bf16),
        jax.ShapeDtypeStruct((b, ATTN_WIDTH, s), bf16),
    ]
    out_specs = [
        pl.BlockSpec((1, tm, CONV_WIDTH), lambda bi, ti: (bi, ti, 0)),
        pl.BlockSpec((1, tm, CONV_WIDTH), lambda bi, ti: (bi, ti, 0)),
        pl.BlockSpec((1, N_HEADS * HEAD_PAD, tm), lambda bi, ti: (bi, 0, ti)),
        pl.BlockSpec((1, N_HEADS, tm, HEAD_PAD), lambda bi, ti: (bi, 0, ti, 0)),
        pl.BlockSpec((1, N_HEADS, V_ROWS, tm), lambda bi, ti: (bi, 0, 0, ti)),
        pl.BlockSpec((1, ATTN_WIDTH, tm), lambda bi, ti: (bi, 0, ti)),
    ]
    q_scale = math.log2(math.e) / math.sqrt(QK_DIM)
    return pl.pallas_call(
        functools.partial(_proj_kernel, q_scale=q_scale),
        grid=grid, in_specs=in_specs, out_specs=out_specs, out_shape=out_shape,
        compiler_params=pltpu.CompilerParams(
            dimension_semantics=("parallel", "parallel"),
            vmem_limit_bytes=VMEM_LIMIT_BYTES),
        name="proj",
    )(x, npre, wconv, wlat, wzt, qnorm, wuqt, kvnorm, wuk, wuvt, cos2, sin2, cos_t, sin_t)


def _attn_kernel(qt_ref, k_ref, vt_ref, gt_ref, o_ref, s_a, s_b, acc_ref, *, bq, bkv, sub):
    s_scr = (s_a, s_b)
    heads = k_ref.shape[1]
    s_len = k_ref.shape[2]
    nkv = s_len // bkv
    nq = s_len // bq
    ntrips = heads * nq * nkv
    assert nkv & (nkv - 1) == 0 and nq & (nq - 1) == 0
    kv_shift = nkv.bit_length() - 1
    q_shift = nq.bit_length() - 1

    def split(n):
        head = (n >> (kv_shift + q_shift)) if heads > 1 else 0
        qoff = pl.multiple_of(((n >> kv_shift) & (nq - 1)) * bq, bq)
        return head, qoff, n & (nkv - 1)

    def rows(head, width):
        return pl.ds(pl.multiple_of(head * width, width), width) if heads > 1 else slice(None)

    def scores(n, bank, j):
        head, qoff, chunk = split(n)
        koff = pl.multiple_of(chunk * bkv + j * sub, sub)
        sc = _dot(k_ref[0, head, pl.ds(koff, sub), :],
                  qt_ref[0, rows(head, HEAD_PAD), pl.ds(qoff, bq)])
        s_scr[bank][j * sub:(j + 1) * sub, :] = sc
        return jnp.max(sc, axis=0, keepdims=True)

    def trip(n, bank, carry, *, last=False):
        m, cmax = carry
        head, qoff, chunk = split(n)
        m = jnp.where(chunk == 0, -jnp.inf, m)
        m_new = jnp.maximum(m, cmax)
        alpha = jnp.exp2(m - m_new)
        nsub = bkv // sub
        cmax_next = cmax if last else scores(n + 1, 1 - bank, 0)
        pv = None
        for j in range(nsub):
            if j + 1 < nsub and not last:
                cmax_next = jnp.maximum(cmax_next, scores(n + 1, 1 - bank, j + 1))
            p = jnp.exp2(s_scr[bank][j * sub:(j + 1) * sub, :] - m_new).astype(jnp.bfloat16)
            koff = pl.multiple_of(chunk * bkv + j * sub, sub)
            part = _dot(vt_ref[0, head, :, pl.ds(koff, sub)], p)
            pv = part if pv is None else pv + part
        acc = alpha * acc_ref[...] + pv
        acc_ref[...] = acc

        @pl.when(chunk == nkv - 1)
        def _():
            denom = acc[V_HEAD_DIM:V_HEAD_DIM + 1]
            gate = gt_ref[0, rows(head, V_HEAD_DIM), pl.ds(qoff, bq)].astype(jnp.float32)
            out = acc[:V_HEAD_DIM] / denom * gate
            o_ref[0, rows(head, V_HEAD_DIM), pl.ds(qoff, bq)] = out.astype(o_ref.dtype)

        return m_new, cmax_next

    acc_ref[...] = jnp.zeros(acc_ref.shape, acc_ref.dtype)
    cmax0 = scores(0, 0, 0)
    for j in range(1, bkv // sub):
        cmax0 = jnp.maximum(cmax0, scores(0, 0, j))
    carry = (jnp.zeros((1, bq), jnp.float32), cmax0)

    def steady(n, carry):
        return lax.cond((n & 1) == 1, lambda c: trip(n, 1, c), lambda c: trip(n, 0, c), carry)

    carry = lax.fori_loop(0, ntrips - 1, steady, carry)
    trip(ntrips - 1, (ntrips - 1) % 2, carry, last=True)


def _attn_call(qt, k, vt, gt, *, bq, bkv, sub, heads, single_buffer):
    b, _, s = qt.shape
    grid = (b, N_HEADS // heads)
    mode = dict(pipeline_mode=pl.Buffered(1)) if single_buffer else {}
    return pl.pallas_call(
        functools.partial(_attn_kernel, bq=bq, bkv=bkv, sub=sub),
        grid=grid,
        in_specs=[
            pl.BlockSpec((1, heads * HEAD_PAD, s), lambda bi, h: (bi, h, 0), **mode),
            pl.BlockSpec((1, heads, s, HEAD_PAD), lambda bi, h: (bi, h, 0, 0), **mode),
            pl.BlockSpec((1, heads, V_ROWS, s), lambda bi, h: (bi, h, 0, 0), **mode),
            pl.BlockSpec((1, heads * V_HEAD_DIM, s), lambda bi, h: (bi, h, 0), **mode),
        ],
        out_specs=pl.BlockSpec((1, heads * V_HEAD_DIM, s), lambda bi, h: (bi, h, 0)),
        out_shape=jax.ShapeDtypeStruct((b, ATTN_WIDTH, s), jnp.bfloat16),
        scratch_shapes=[pltpu.VMEM((bkv, bq), jnp.float32)] * 2
        + [pltpu.VMEM((V_ROWS, bq), jnp.float32)],
        compiler_params=pltpu.CompilerParams(
            dimension_semantics=("parallel", "parallel"),
            vmem_limit_bytes=VMEM_LIMIT_BYTES),
        name="attn",
    )(qt, k, vt, gt)


def _out_kernel(x_ref, cu_ref, cup_ref, cun_ref, g_ref, at_ref, convw_ref, wa_ref, wb_ref,
                npost_ref, y_ref):
    ti = pl.program_id(1)
    nt = pl.num_programs(1)
    cu = cu_ref[0].astype(jnp.float32)
    tm = cu.shape[0]
    prev_row = cup_ref[0, BF16_SUBLANES - 1:BF16_SUBLANES, :].astype(jnp.float32)
    next_row = cun_ref[0, 0:1, :].astype(jnp.float32)
    prev_row = jnp.where(ti > 0, prev_row, 0.0)
    next_row = jnp.where(ti < nt - 1, next_row, 0.0)
    row = lax.broadcasted_iota(jnp.int32, cu.shape, 0)
    up = jnp.where(row == 0, prev_row, pltpu.roll(cu, 1, 0))
    dn = jnp.where(row == tm - 1, next_row, pltpu.roll(cu, tm - 1, 0))
    w = convw_ref[...]
    conv = up * w[0:1] + cu * w[1:2] + dn * w[2:3]
    conv_out = (conv * g_ref[0].astype(jnp.float32)).astype(jnp.bfloat16)
    out = _dot(conv_out, wa_ref[...])
    out = out + lax.dot_general(at_ref[0], wb_ref[...], TN_DIMS,
                                preferred_element_type=jnp.float32)
    y_ref[0] = x_ref[0] + _rms(out, npost_ref[...])


def _out_call(x, cu, g, at, conv_w, wa, wb, npost, *, tm):
    b, s, d = x.shape
    grid = (b, s // tm)
    halo = BF16_SUBLANES
    nb = tm // halo
    last = s // halo - 1
    full = lambda arr: pl.BlockSpec(arr.shape, lambda bi, ti: (0,) * arr.ndim)
    return pl.pallas_call(
        _out_kernel,
        grid=grid,
        in_specs=[
            pl.BlockSpec((1, tm, d), lambda bi, ti: (bi, ti, 0)),
            pl.BlockSpec((1, tm, CONV_WIDTH), lambda bi, ti: (bi, ti, 0)),
            pl.BlockSpec((1, halo, CONV_WIDTH),
                         lambda bi, ti: (bi, jnp.maximum(ti * nb - 1, 0), 0)),
            pl.BlockSpec((1, halo, CONV_WIDTH),
                         lambda bi, ti: (bi, jnp.minimum((ti + 1) * nb, last), 0)),
            pl.BlockSpec((1, tm, CONV_WIDTH), lambda bi, ti: (bi, ti, 0)),
            pl.BlockSpec((1, ATTN_WIDTH, tm), lambda bi, ti: (bi, 0, ti)),
            full(conv_w), full(wa), full(wb), full(npost),
        ],
        out_specs=pl.BlockSpec((1, tm, d), lambda bi, ti: (bi, ti, 0)),
        out_shape=jax.ShapeDtypeStruct((b, s, d), x.dtype),
        compiler_params=pltpu.CompilerParams(
            dimension_semantics=("parallel", "parallel"),
            vmem_limit_bytes=VMEM_LIMIT_BYTES),
        name="outproj",
    )(x, cu, cu, cu, g, at, conv_w, wa, wb, npost)


def _prep_weights(norm_pre, w_in, q_norm, w_uq, kv_norm, w_ukv):
    bf16 = jnp.bfloat16
    wconv = w_in[:, :OFF_QLAT].astype(bf16)
    wzt = w_in[:, OFF_ZATTN:].T.astype(bf16)
    kpe = w_in[:, OFF_KPE:OFF_ZATTN]
    kpe_sw = jnp.concatenate([kpe[:, HALF_ROPE:], kpe[:, :HALF_ROPE]], axis=1)
    pad = jnp.zeros((D_MODEL, QK_ROPE_DIM), w_in.dtype)
    wlat = jnp.concatenate([w_in[:, OFF_QLAT:OFF_KPE], pad, kpe_sw, kpe, pad], axis=1).astype(bf16)
    wuq = w_uq.reshape(Q_LORA_RANK, N_HEADS, QK_DIM)
    wuq = jnp.pad(wuq, ((0, 0), (0, 0), (0, HEAD_PAD - QK_DIM)))
    wuqt = wuq.reshape(Q_LORA_RANK, N_HEADS * HEAD_PAD).T.astype(bf16)
    wukv = w_ukv.reshape(KV_LORA_RANK, N_HEADS, QK_NOPE_DIM + V_HEAD_DIM)
    wuk = jnp.pad(wukv[:, :, :QK_NOPE_DIM], ((0, 0), (0, 0), (0, HEAD_PAD - QK_NOPE_DIM)))
    wuk = wuk.reshape(KV_LORA_RANK, N_HEADS * HEAD_PAD).astype(bf16)
    wuvt = wukv[:, :, QK_NOPE_DIM:].reshape(KV_LORA_RANK, ATTN_WIDTH).T.astype(bf16)
    return (norm_pre.reshape(1, -1), wconv, wlat, wzt, q_norm.reshape(1, -1), wuqt,
            kv_norm.reshape(1, -1), wuk, wuvt)


def _rope_tables(seq_len):
    freqs = 1.0 / (ROPE_THETA ** (jnp.arange(0, QK_ROPE_DIM, 2, dtype=jnp.float32) / QK_ROPE_DIM))
    ang = jnp.arange(seq_len, dtype=jnp.float32)[:, None] * freqs[None, :]
    cos, sin = jnp.cos(ang), jnp.sin(ang)
    zl = jnp.zeros((seq_len, QK_NOPE_DIM), jnp.float32)
    zr = jnp.zeros((seq_len, HEAD_PAD - QK_DIM), jnp.float32)
    cos2 = jnp.concatenate([zl, cos, cos, zr], axis=1)
    sin2 = jnp.concatenate([zl, -sin, sin, zr], axis=1)
    return cos2, sin2, cos.T, sin.T


def _layer(x, weights, tables, conv_w, wa, wb, npost, *, tm_proj, bq, bkv, sub, tm_out):
    cu, g, qt, k, vt, gt = _proj_call(x, weights, tables, tm=tm_proj)
    bkv = min(bkv, x.shape[1])
    heads = max(1, min(N_HEADS, ATTN_WINDOW_ROWS // x.shape[1]))
    at = _attn_call(qt, k, vt, gt, bq=bq, bkv=bkv, sub=min(sub, bkv), heads=heads,
                    single_buffer=2 * bkv * bq * 4 > SCORE_BANKS_DOUBLE_BUFFER_LIMIT)
    return _out_call(x, cu, g, at, conv_w, wa, wb, npost, tm=tm_out)


def kernel(x_prompt, x_sample, norm_pre, w_in, conv_w, q_norm, w_uq, kv_norm, w_ukv, w_out, norm_post):
    y_prompt, y_sample = x_prompt, x_sample
    tables = _rope_tables(max(x_prompt.shape[1], x_sample.shape[1]))
    for l in range(norm_pre.shape[0]):
        weights = _prep_weights(norm_pre[l], w_in[l], q_norm[l], w_uq[l], kv_norm[l], w_ukv[l])
        wa = w_out[l, :CONV_WIDTH].astype(jnp.bfloat16)
        wb = w_out[l, CONV_WIDTH:].astype(jnp.bfloat16)
        npost = norm_post[l].reshape(1, -1)
        cfg = dict(tm_proj=512, bq=512, bkv=8192, sub=256, tm_out=1024)
        y_prompt = _layer(y_prompt, weights, tables, conv_w[l], wa, wb, npost, **cfg)
        y_sample = _layer(y_sample, weights, tables, conv_w[l], wa, wb, npost, **cfg)
    return (y_prompt, y_sample)
```

```python
import functools
import math

import jax
import jax.numpy as jnp
from jax import lax
from jax.experimental import pallas as pl
from jax.experimental.pallas import tpu as pltpu

D_MODEL = 1024
CONV_WIDTH = 512
N_HEADS = 8
QK_NOPE_DIM = 64
QK_ROPE_DIM = 32
HALF_ROPE = QK_ROPE_DIM // 2
V_HEAD_DIM = 64
ATTN_WIDTH = N_HEADS * V_HEAD_DIM
Q_LORA_RANK = 384
KV_LORA_RANK = 256
ROPE_THETA = 10000.0
NORM_EPS = 1e-6
QK_DIM = QK_NOPE_DIM + QK_ROPE_DIM
HEAD_PAD = 128
V_ROWS = 80
LAT_WIDTH = Q_LORA_RANK + KV_LORA_RANK
OFF_QLAT = 4 * CONV_WIDTH
OFF_KVLAT = OFF_QLAT + Q_LORA_RANK
OFF_KPE = OFF_KVLAT + KV_LORA_RANK
OFF_ZATTN = OFF_KPE + QK_ROPE_DIM

BF16_SUBLANES = 16
VMEM_LIMIT_BYTES = 56 * 1024 * 1024
SCORE_BANKS_DOUBLE_BUFFER_LIMIT = 16 * 1024 * 1024
ATTN_WINDOW_ROWS = 16384

NT_DIMS = (((1,), (1,)), ((), ()))
TN_DIMS = (((0,), (0,)), ((), ()))


def _rms(x, g):
    return x * lax.rsqrt(jnp.mean(x * x, axis=-1, keepdims=True) + NORM_EPS) * g


def _silu(x):
    return x * (1.0 / (1.0 + jnp.exp(-x)))


def _dot(a, b):
    return jnp.dot(a, b, preferred_element_type=jnp.float32)


def _proj_kernel(x_ref, npre_ref, wconv_ref, wlat_ref, wzt_ref, qnorm_ref,
                 wuqt_ref, kvnorm_ref, wuk_ref, wuvt_ref, cos2_ref, sin2_ref, cost_ref,
                 sint_ref, cu_ref, g_ref, qt_ref, k_ref, vt_ref, gt_ref, *, q_scale):
    bf16 = jnp.bfloat16
    x = x_ref[0]
    tm = x.shape[0]
    hdn = _rms(x, npre_ref[...]).astype(bf16)

    pc = _dot(hdn, wconv_ref[...])
    cw = CONV_WIDTH
    cu_ref[0] = (pc[:, 2 * cw:3 * cw] * pc[:, 0:cw]).astype(bf16)
    g_ref[0] = (pc[:, cw:2 * cw] * _silu(pc[:, 3 * cw:4 * cw])).astype(bf16)

    lat = _dot(hdn, wlat_ref[...])
    qn = _rms(lat[:, :Q_LORA_RANK], qnorm_ref[...]).astype(bf16)
    kvn = _rms(lat[:, Q_LORA_RANK:LAT_WIDTH], kvnorm_ref[...]).astype(bf16)

    qt = lax.dot_general(wuqt_ref[...], qn, NT_DIMS, preferred_element_type=jnp.float32)
    qt = qt * q_scale
    qt_ref[0] = qt.astype(bf16)
    cos_t = cost_ref[...]
    sin_t = sint_ref[...]
    for h in range(N_HEADS):
        r1 = h * HEAD_PAD + QK_NOPE_DIM
        r2 = r1 + HALF_ROPE
        x1 = qt[r1:r2]
        x2 = qt[r2:r2 + HALF_ROPE]
        qt_ref[0, r1:r2, :] = (x1 * cos_t - x2 * sin_t).astype(bf16)
        qt_ref[0, r2:r2 + HALF_ROPE, :] = (x2 * cos_t + x1 * sin_t).astype(bf16)

    kn = _dot(kvn, wuk_ref[...])
    kp = lat[:, LAT_WIDTH:]
    roped = kp * cos2_ref[...] + pltpu.roll(kp, QK_ROPE_DIM, 1) * sin2_ref[...]
    low_half = lax.broadcasted_iota(jnp.int32, (tm, HEAD_PAD), 1) < QK_NOPE_DIM
    for h in range(0, N_HEADS, 2):
        pair = kn[:, h * QK_NOPE_DIM:(h + 2) * QK_NOPE_DIM]
        k_ref[0, h] = jnp.where(low_half, pair, roped).astype(bf16)
        odd = pltpu.roll(pair, QK_NOPE_DIM, 1)
        k_ref[0, h + 1] = jnp.where(low_half, odd, roped).astype(bf16)

    vt = lax.dot_general(wuvt_ref[...], kvn, NT_DIMS, preferred_element_type=jnp.float32)
    row = lax.broadcasted_iota(jnp.int32, (V_ROWS - V_HEAD_DIM, tm), 0)
    ones_rows = jnp.where(row == 0, 1.0, 0.0).astype(bf16)
    for h in range(N_HEADS):
        vt_ref[0, h, 0:V_HEAD_DIM, :] = vt[h * V_HEAD_DIM:(h + 1) * V_HEAD_DIM].astype(bf16)
        vt_ref[0, h, V_HEAD_DIM:V_ROWS, :] = ones_rows

    zt = lax.dot_general(wzt_ref[...], hdn, NT_DIMS, preferred_element_type=jnp.float32)
    gt_ref[0] = _silu(zt).astype(bf16)


def _proj_call(x, weights, tables, *, tm):
    b, s, d = x.shape
    bf16 = jnp.bfloat16
    grid = (b, s // tm)
    full = lambda arr: pl.BlockSpec(arr.shape, lambda bi, ti: (0,) * arr.ndim)
    (npre, wconv, wlat, wzt, qnorm, wuqt, kvnorm, wuk, wuvt) = weights
    cos2, sin2, cos_t, sin_t = tables
    in_specs = [
        pl.BlockSpec((1, tm, d), lambda bi, ti: (bi, ti, 0)),
        full(npre), full(wconv), full(wlat), full(wzt), full(qnorm),
        full(wuqt), full(kvnorm), full(wuk), full(wuvt),
        pl.BlockSpec((tm, HEAD_PAD), lambda bi, ti: (ti, 0)),
        pl.BlockSpec((tm, HEAD_PAD), lambda bi, ti: (ti, 0)),
        pl.BlockSpec((HALF_ROPE, tm), lambda bi, ti: (0, ti)),
        pl.BlockSpec((HALF_ROPE, tm), lambda bi, ti: (0, ti)),
    ]
    out_shape = [
        jax.ShapeDtypeStruct((b, s, CONV_WIDTH), bf16),
        jax.ShapeDtypeStruct((b, s, CONV_WIDTH), bf16),
        jax.ShapeDtypeStruct((b, N_HEADS * HEAD_PAD, s), bf16),
        jax.ShapeDtypeStruct((b, N_HEADS, s, HEAD_PAD), bf16),
        jax.ShapeDtypeStruct((b, N_HEADS, V_ROWS, s), bf16),
        jax.ShapeDtypeStruct((b, ATTN_WIDTH, s), bf16),
    ]
    out_specs = [
        pl.BlockSpec((1, tm, CONV_WIDTH), lambda bi, ti: (bi, ti, 0)),
        pl.BlockSpec((1, tm, CONV_WIDTH), lambda bi, ti: (bi, ti, 0)),
        pl.BlockSpec((1, N_HEADS * HEAD_PAD, tm), lambda bi, ti: (bi, 0, ti)),
        pl.BlockSpec((1, N_HEADS, tm, HEAD_PAD), lambda bi, ti: (bi, 0, ti, 0)),
        pl.BlockSpec((1, N_HEADS, V_ROWS, tm), lambda bi, ti: (bi, 0, 0, ti)),
        pl.BlockSpec((1, ATTN_WIDTH, tm), lambda bi, ti: (bi, 0, ti)),
    ]
    q_scale = math.log2(math.e) / math.sqrt(QK_DIM)
    return pl.pallas_call(
        functools.partial(_proj_kernel, q_scale=q_scale),
        grid=grid, in_specs=in_specs, out_specs=out_specs, out_shape=out_shape,
        compiler_params=pltpu.CompilerParams(
            dimension_semantics=("parallel", "parallel"),
            vmem_limit_bytes=VMEM_LIMIT_BYTES),
        name="proj",
    )(x, npre, wconv, wlat, wzt, qnorm, wuqt, kvnorm, wuk, wuvt, cos2, sin2, cos_t, sin_t)


def _attn_kernel(qt_ref, k_ref, vt_ref, gt_ref, o_ref, s_a, s_b, acc_ref, *, bq, bkv, sub):
    s_scr = (s_a, s_b)
    heads = k_ref.shape[1]
    s_len = k_ref.shape[2]
    nkv = s_len // bkv
    nq = s_len // bq
    ntrips = heads * nq * nkv
    assert nkv & (nkv - 1) == 0 and nq & (nq - 1) == 0
    kv_shift = nkv.bit_length() - 1
    q_shift = nq.bit_length() - 1

    def split(n):
        head = (n >> (kv_shift + q_shift)) if heads > 1 else 0
        qoff = pl.multiple_of(((n >> kv_shift) & (nq - 1)) * bq, bq)
        return head, qoff, n & (nkv - 1)

    def rows(head, width):
        return pl.ds(pl.multiple_of(head * width, width), width) if heads > 1 else slice(None)

    def scores(n, bank, j):
        head, qoff, chunk = split(n)
        koff = pl.multiple_of(chunk * bkv + j * sub, sub)
        sc = _dot(k_ref[0, head, pl.ds(koff, sub), :],
                  qt_ref[0, rows(head, HEAD_PAD), pl.ds(qoff, bq)])
        s_scr[bank][j * sub:(j + 1) * sub, :] = sc
        return jnp.max(sc, axis=0, keepdims=True)

    def trip(n, bank, carry, *, last=False):
        m, cmax = carry
        head, qoff, chunk = split(n)
        m = jnp.where(chunk == 0, -jnp.inf, m)
        m_new = jnp.maximum(m, cmax)
        alpha = jnp.exp2(m - m_new)
        nsub = bkv // sub
        cmax_next = cmax if last else scores(n + 1, 1 - bank, 0)
        pv = None
        for j in range(nsub):
            if j + 1 < nsub and not last:
                cmax_next = jnp.maximum(cmax_next, scores(n + 1, 1 - bank, j + 1))
            p = jnp.exp2(s_scr[bank][j * sub:(j + 1) * sub, :] - m_new).astype(jnp.bfloat16)
            koff = pl.multiple_of(chunk * bkv + j * sub, sub)
            part = _dot(vt_ref[0, head, :, pl.ds(koff, sub)], p)
            pv = part if pv is None else pv + part
        acc = alpha * acc_ref[...] + pv
        acc_ref[...] = acc

        @pl.when(chunk == nkv - 1)
        def _():
            denom = acc[V_HEAD_DIM:V_HEAD_DIM + 1]
            gate = gt_ref[0, rows(head, V_HEAD_DIM), pl.ds(qoff, bq)].astype(jnp.float32)
            out = acc[:V_HEAD_DIM] / denom * gate
            o_ref[0, rows(head, V_HEAD_DIM), pl.ds(qoff, bq)] = out.astype(o_ref.dtype)

        return m_new, cmax_next

    acc_ref[...] = jnp.zeros(acc_ref.shape, acc_ref.dtype)
    cmax0 = scores(0, 0, 0)
    for j in range(1, bkv // sub):
        cmax0 = jnp.maximum(cmax0, scores(0, 0, j))
    carry = (jnp.zeros((1, bq), jnp.float32), cmax0)

    assert ntrips % 2 == 0

    def two_trips(i, carry):
        return trip(2 * i + 1, 1, trip(2 * i, 0, carry))

    carry = lax.fori_loop(0, ntrips // 2 - 1, two_trips, carry)
    trip(ntrips - 1, 1, trip(ntrips - 2, 0, carry), last=True)


def _attn_call(qt, k, vt, gt, *, bq, bkv, sub, heads, single_buffer):
    b, _, s = qt.shape
    grid = (b, N_HEADS // heads)
    mode = dict(pipeline_mode=pl.Buffered(1)) if single_buffer else {}
    return pl.pallas_call(
        functools.partial(_attn_kernel, bq=bq, bkv=bkv, sub=sub),
        grid=grid,
        in_specs=[
            pl.BlockSpec((1, heads * HEAD_PAD, s), lambda bi, h: (bi, h, 0), **mode),
            pl.BlockSpec((1, heads, s, HEAD_PAD), lambda bi, h: (bi, h, 0, 0), **mode),
            pl.BlockSpec((1, heads, V_ROWS, s), lambda bi, h: (bi, h, 0, 0), **mode),
            pl.BlockSpec((1, heads * V_HEAD_DIM, s), lambda bi, h: (bi, h, 0), **mode),
        ],
        out_specs=pl.BlockSpec((1, heads * V_HEAD_DIM, s), lambda bi, h: (bi, h, 0)),
        out_shape=jax.ShapeDtypeStruct((b, ATTN_WIDTH, s), jnp.bfloat16),
        scratch_shapes=[pltpu.VMEM((bkv, bq), jnp.float32)] * 2
        + [pltpu.VMEM((V_ROWS, bq), jnp.float32)],
        compiler_params=pltpu.CompilerParams(
            dimension_semantics=("parallel", "parallel"),
            vmem_limit_bytes=VMEM_LIMIT_BYTES),
        name="attn",
    )(qt, k, vt, gt)


def _out_kernel(x_ref, cu_ref, cup_ref, cun_ref, g_ref, at_ref, convw_ref, wa_ref, wb_ref,
                npost_ref, y_ref):
    ti = pl.program_id(1)
    nt = pl.num_programs(1)
    cu = cu_ref[0].astype(jnp.float32)
    tm = cu.shape[0]
    prev_row = cup_ref[0, BF16_SUBLANES - 1:BF16_SUBLANES, :].astype(jnp.float32)
    next_row = cun_ref[0, 0:1, :].astype(jnp.float32)
    prev_row = jnp.where(ti > 0, prev_row, 0.0)
    next_row = jnp.where(ti < nt - 1, next_row, 0.0)
    row = lax.broadcasted_iota(jnp.int32, cu.shape, 0)
    up = jnp.where(row == 0, prev_row, pltpu.roll(cu, 1, 0))
    dn = jnp.where(row == tm - 1, next_row, pltpu.roll(cu, tm - 1, 0))
    w = convw_ref[...]
    conv = up * w[0:1] + cu * w[1:2] + dn * w[2:3]
    conv_out = (conv * g_ref[0].astype(jnp.float32)).astype(jnp.bfloat16)
    out = _dot(conv_out, wa_ref[...])
    out = out + lax.dot_general(at_ref[0], wb_ref[...], TN_DIMS,
                                preferred_element_type=jnp.float32)
    y_ref[0] = x_ref[0] + _rms(out, npost_ref[...])


def _out_call(x, cu, g, at, conv_w, wa, wb, npost, *, tm):
    b, s, d = x.shape
    grid = (b, s // tm)
    halo = BF16_SUBLANES
    nb = tm // halo
    last = s // halo - 1
    full = lambda arr: pl.BlockSpec(arr.shape, lambda bi, ti: (0,) * arr.ndim)
    return pl.pallas_call(
        _out_kernel,
        grid=grid,
        in_specs=[
            pl.BlockSpec((1, tm, d), lambda bi, ti: (bi, ti, 0)),
            pl.BlockSpec((1, tm, CONV_WIDTH), lambda bi, ti: (bi, ti, 0)),
            pl.BlockSpec((1, halo, CONV_WIDTH),
                         lambda bi, ti: (bi, jnp.maximum(ti * nb - 1, 0), 0)),
            pl.BlockSpec((1, halo, CONV_WIDTH),
                         lambda bi, ti: (bi, jnp.minimum((ti + 1) * nb, last), 0)),
            pl.BlockSpec((1, tm, CONV_WIDTH), lambda bi, ti: (bi, ti, 0)),
            pl.BlockSpec((1, ATTN_WIDTH, tm), lambda bi, ti: (bi, 0, ti)),
            full(conv_w), full(wa), full(wb), full(npost),
        ],
        out_specs=pl.BlockSpec((1, tm, d), lambda bi, ti: (bi, ti, 0)),
        out_shape=jax.ShapeDtypeStruct((b, s, d), x.dtype),
        compiler_params=pltpu.CompilerParams(
            dimension_semantics=("parallel", "parallel"),
            vmem_limit_bytes=VMEM_LIMIT_BYTES),
        name="outproj",
    )(x, cu, cu, cu, g, at, conv_w, wa, wb, npost)


def _prep_weights(norm_pre, w_in, q_norm, w_uq, kv_norm, w_ukv):
    bf16 = jnp.bfloat16
    wconv = w_in[:, :OFF_QLAT].astype(bf16)
    wzt = w_in[:, OFF_ZATTN:].T.astype(bf16)
    kpe = w_in[:, OFF_KPE:OFF_ZATTN]
    kpe_sw = jnp.concatenate([kpe[:, HALF_ROPE:], kpe[:, :HALF_ROPE]], axis=1)
    pad = jnp.zeros((D_MODEL, QK_ROPE_DIM), w_in.dtype)
    wlat = jnp.concatenate([w_in[:, OFF_QLAT:OFF_KPE], pad, kpe_sw, kpe, pad], axis=1).astype(bf16)
    wuq = w_uq.reshape(Q_LORA_RANK, N_HEADS, QK_DIM)
    wuq = jnp.pad(wuq, ((0, 0), (0, 0), (0, HEAD_PAD - QK_DIM)))
    wuqt = wuq.reshape(Q_LORA_RANK, N_HEADS * HEAD_PAD).T.astype(bf16)
    wukv = w_ukv.reshape(KV_LORA_RANK, N_HEADS, QK_NOPE_DIM + V_HEAD_DIM)
    wuk = wukv[:, :, :QK_NOPE_DIM].reshape(KV_LORA_RANK, N_HEADS * QK_NOPE_DIM).astype(bf16)
    wuvt = wukv[:, :, QK_NOPE_DIM:].reshape(KV_LORA_RANK, ATTN_WIDTH).T.astype(bf16)
    return (norm_pre.reshape(1, -1), wconv, wlat, wzt, q_norm.reshape(1, -1), wuqt,
            kv_norm.reshape(1, -1), wuk, wuvt)


def _rope_tables(seq_len):
    freqs = 1.0 / (ROPE_THETA ** (jnp.arange(0, QK_ROPE_DIM, 2, dtype=jnp.float32) / QK_ROPE_DIM))
    ang_t = freqs[:, None] * jnp.arange(seq_len, dtype=jnp.float32)[None, :]
    cos_t, sin_t = jnp.cos(ang_t), jnp.sin(ang_t)
    zl = jnp.zeros((QK_NOPE_DIM, seq_len), jnp.float32)
    zr = jnp.zeros((HEAD_PAD - QK_DIM, seq_len), jnp.float32)
    cos2 = jnp.concatenate([zl, cos_t, cos_t, zr], axis=0).T
    sin2 = jnp.concatenate([zl, -sin_t, sin_t, zr], axis=0).T
    return cos2, sin2, cos_t, sin_t


def _layer(x, weights, tables, conv_w, wa, wb, npost, *, tm_proj, bq, bkv, sub, tm_out):
    cu, g, qt, k, vt, gt = _proj_call(x, weights, tables, tm=tm_proj)
    bkv = min(bkv, x.shape[1])
    heads = max(1, min(N_HEADS, ATTN_WINDOW_ROWS // x.shape[1]))
    at = _attn_call(qt, k, vt, gt, bq=bq, bkv=bkv, sub=min(sub, bkv), heads=heads,
                    single_buffer=2 * bkv * bq * 4 > SCORE_BANKS_DOUBLE_BUFFER_LIMIT)
    return _out_call(x, cu, g, at, conv_w, wa, wb, npost, tm=tm_out)


def kernel(x_prompt, x_sample, norm_pre, w_in, conv_w, q_norm, w_uq, kv_norm, w_ukv, w_out, norm_post):
    y_prompt, y_sample = x_prompt, x_sample
    tables = _rope_tables(max(x_prompt.shape[1], x_sample.shape[1]))
    for l in range(norm_pre.shape[0]):
        weights = _prep_weights(norm_pre[l], w_in[l], q_norm[l], w_uq[l], kv_norm[l], w_ukv[l])
        wa = w_out[l, :CONV_WIDTH].astype(jnp.bfloat16)
        wb = w_out[l, CONV_WIDTH:].astype(jnp.bfloat16)
        npost = norm_post[l].reshape(1, -1)
        cfg = dict(tm_proj=512, bkv=8192, sub=256, tm_out=1024)
        y_prompt = _layer(y_prompt, weights, tables, conv_w[l], wa, wb, npost, bq=512, **cfg)
        y_sample = _layer(y_sample, weights, tables, conv_w[l], wa, wb, npost, bq=512, **cfg)
    return (y_prompt, y_sample)
```

```python
import functools
import math

import jax
import jax.numpy as jnp
from jax import lax
from jax.experimental import pallas as pl
from jax.experimental.pallas import tpu as pltpu

D_MODEL = 1024
CONV_WIDTH = 512
N_HEADS = 8
QK_NOPE_DIM = 64
QK_ROPE_DIM = 32
HALF_ROPE = QK_ROPE_DIM // 2
V_HEAD_DIM = 64
ATTN_WIDTH = N_HEADS * V_HEAD_DIM
Q_LORA_RANK = 384
KV_LORA_RANK = 256
ROPE_THETA = 10000.0
NORM_EPS = 1e-6
QK_DIM = QK_NOPE_DIM + QK_ROPE_DIM
HEAD_PAD = 128
V_ROWS = 80
LAT_WIDTH = Q_LORA_RANK + KV_LORA_RANK
OFF_QLAT = 4 * CONV_WIDTH
OFF_KVLAT = OFF_QLAT + Q_LORA_RANK
OFF_KPE = OFF_KVLAT + KV_LORA_RANK
OFF_ZATTN = OFF_KPE + QK_ROPE_DIM

BF16_SUBLANES = 16
VMEM_LIMIT_BYTES = 60 * 1024 * 1024
SCORE_BANKS_DOUBLE_BUFFER_LIMIT = 16 * 1024 * 1024
ATTN_WINDOW_ROWS = 16384

NT_DIMS = (((1,), (1,)), ((), ()))
TN_DIMS = (((0,), (0,)), ((), ()))


def _rms(x, g):
    return x * lax.rsqrt(jnp.mean(x * x, axis=-1, keepdims=True) + NORM_EPS) * g


def _silu(x):
    return x * (1.0 / (1.0 + jnp.exp(-x)))


def _dot(a, b):
    return jnp.dot(a, b, preferred_element_type=jnp.float32)


def _proj_kernel(x_ref, npre_ref, wconv_ref, wlat_ref, wzt_ref, qnorm_ref,
                 wuqt_ref, kvnorm_ref, wuk_ref, wuvt_ref, cos2_ref, sin2_ref, cost_ref,
                 sint_ref, cu_ref, g_ref, qt_ref, k_ref, vt_ref, gt_ref, *, q_scale):
    bf16 = jnp.bfloat16
    x = x_ref[0]
    tm = x.shape[0]
    hdn = _rms(x, npre_ref[...]).astype(bf16)

    pc = _dot(hdn, wconv_ref[...])
    cw = CONV_WIDTH
    cu_ref[0] = (pc[:, 2 * cw:3 * cw] * pc[:, 0:cw]).astype(bf16)
    g_ref[0] = (pc[:, cw:2 * cw] * _silu(pc[:, 3 * cw:4 * cw])).astype(bf16)

    lat = _dot(hdn, wlat_ref[...])
    qn = _rms(lat[:, :Q_LORA_RANK], qnorm_ref[...]).astype(bf16)
    kvn = _rms(lat[:, Q_LORA_RANK:LAT_WIDTH], kvnorm_ref[...]).astype(bf16)

    qt = lax.dot_general(wuqt_ref[...], qn, NT_DIMS, preferred_element_type=jnp.float32)
    qt = qt * q_scale
    qt_ref[0] = qt.astype(bf16)
    cos_t = cost_ref[...]
    sin_t = sint_ref[...]
    for h in range(N_HEADS):
        r1 = h * HEAD_PAD + QK_NOPE_DIM
        r2 = r1 + HALF_ROPE
        x1 = qt[r1:r2]
        x2 = qt[r2:r2 + HALF_ROPE]
        qt_ref[0, r1:r2, :] = (x1 * cos_t - x2 * sin_t).astype(bf16)
        qt_ref[0, r2:r2 + HALF_ROPE, :] = (x2 * cos_t + x1 * sin_t).astype(bf16)

    kn = _dot(kvn, wuk_ref[...])
    kp = lat[:, LAT_WIDTH:]
    roped = kp * cos2_ref[...] + pltpu.roll(kp, QK_ROPE_DIM, 1) * sin2_ref[...]
    low_half = lax.broadcasted_iota(jnp.int32, (tm, HEAD_PAD), 1) < QK_NOPE_DIM
    for h in range(0, N_HEADS, 2):
        pair = kn[:, h * QK_NOPE_DIM:(h + 2) * QK_NOPE_DIM]
        k_ref[0, h] = jnp.where(low_half, pair, roped).astype(bf16)
        odd = pltpu.roll(pair, QK_NOPE_DIM, 1)
        k_ref[0, h + 1] = jnp.where(low_half, odd, roped).astype(bf16)

    vt = lax.dot_general(wuvt_ref[...], kvn, NT_DIMS, preferred_element_type=jnp.float32)
    row = lax.broadcasted_iota(jnp.int32, (V_ROWS - V_HEAD_DIM, tm), 0)
    ones_rows = jnp.where(row == 0, 1.0, 0.0).astype(bf16)
    for h in range(N_HEADS):
        vt_ref[0, h, 0:V_HEAD_DIM, :] = vt[h * V_HEAD_DIM:(h + 1) * V_HEAD_DIM].astype(bf16)
        vt_ref[0, h, V_HEAD_DIM:V_ROWS, :] = ones_rows

    zt = lax.dot_general(wzt_ref[...], hdn, NT_DIMS, preferred_element_type=jnp.float32)
    gt_ref[0] = _silu(zt).astype(bf16)


def _proj_call(x, weights, tables, *, tm):
    b, s, d = x.shape
    bf16 = jnp.bfloat16
    grid = (b, s // tm)
    full = lambda arr: pl.BlockSpec(arr.shape, lambda bi, ti: (0,) * arr.ndim)
    (npre, wconv, wlat, wzt, qnorm, wuqt, kvnorm, wuk, wuvt) = weights
    cos2, sin2, cos_t, sin_t = tables
    in_specs = [
        pl.BlockSpec((1, tm, d), lambda bi, ti: (bi, ti, 0)),
        full(npre), full(wconv), full(wlat), full(wzt), full(qnorm),
        full(wuqt), full(kvnorm), full(wuk), full(wuvt),
        pl.BlockSpec((tm, HEAD_PAD), lambda bi, ti: (ti, 0)),
        pl.BlockSpec((tm, HEAD_PAD), lambda bi, ti: (ti, 0)),
        pl.BlockSpec((HALF_ROPE, tm), lambda bi, ti: (0, ti)),
        pl.BlockSpec((HALF_ROPE, tm), lambda bi, ti: (0, ti)),
    ]
    out_shape = [
        jax.ShapeDtypeStruct((b, s, CONV_WIDTH), bf16),
        jax.ShapeDtypeStruct((b, s, CONV_WIDTH), bf16),
        jax.ShapeDtypeStruct((b, N_HEADS * HEAD_PAD, s), bf16),
        jax.ShapeDtypeStruct((b, N_HEADS, s, HEAD_PAD), bf16),
        jax.ShapeDtypeStruct((b, N_HEADS, V_ROWS, s), bf16),
        jax.ShapeDtypeStruct((b, ATTN_WIDTH, s), bf16),
    ]
    out_specs = [
        pl.BlockSpec((1, tm, CONV_WIDTH), lambda bi, ti: (bi, ti, 0)),
        pl.BlockSpec((1, tm, CONV_WIDTH), lambda bi, ti: (bi, ti, 0)),
        pl.BlockSpec((1, N_HEADS * HEAD_PAD, tm), lambda bi, ti: (bi, 0, ti)),
        pl.BlockSpec((1, N_HEADS, tm, HEAD_PAD), lambda bi, ti: (bi, 0, ti, 0)),
        pl.BlockSpec((1, N_HEADS, V_ROWS, tm), lambda bi, ti: (bi, 0, 0, ti)),
        pl.BlockSpec((1, ATTN_WIDTH, tm), lambda bi, ti: (bi, 0, ti)),
    ]
    q_scale = math.log2(math.e) / math.sqrt(QK_DIM)
    return pl.pallas_call(
        functools.partial(_proj_kernel, q_scale=q_scale),
        grid=grid, in_specs=in_specs, out_specs=out_specs, out_shape=out_shape,
        compiler_params=pltpu.CompilerParams(
            dimension_semantics=("parallel", "parallel"),
            vmem_limit_bytes=VMEM_LIMIT_BYTES),
        name="proj",
    )(x, npre, wconv, wlat, wzt, qnorm, wuqt, kvnorm, wuk, wuvt, cos2, sin2, cos_t, sin_t)


def _attn_kernel(qt_ref, k_ref, vt_ref, gt_ref, o_ref, s_a, s_b, acc_ref, *, bq, bkv, sub, tpb):
    s_scr = (s_a, s_b)
    heads = k_ref.shape[1]
    s_len = k_ref.shape[2]
    nkv = s_len // bkv
    nq = s_len // bq
    ntrips = heads * nq * nkv
    assert nkv & (nkv - 1) == 0 and nq & (nq - 1) == 0
    kv_shift = nkv.bit_length() - 1
    q_shift = nq.bit_length() - 1

    def split(n):
        head = (n >> (kv_shift + q_shift)) if heads > 1 else 0
        qoff = pl.multiple_of(((n >> kv_shift) & (nq - 1)) * bq, bq)
        return head, qoff, n & (nkv - 1)

    def rows(head, width):
        return pl.ds(pl.multiple_of(head * width, width), width) if heads > 1 else slice(None)

    def scores(n, bank, j):
        head, qoff, chunk = split(n)
        koff = pl.multiple_of(chunk * bkv + j * sub, sub)
        sc = _dot(k_ref[0, head, pl.ds(koff, sub), :],
                  qt_ref[0, rows(head, HEAD_PAD), pl.ds(qoff, bq)])
        s_scr[bank][j * sub:(j + 1) * sub, :] = sc
        return jnp.max(sc, axis=0, keepdims=True)

    def trip(n, bank, carry, *, last=False):
        m, cmax = carry
        head, qoff, chunk = split(n)
        m = jnp.where(chunk == 0, -jnp.inf, m)
        m_new = jnp.maximum(m, cmax)
        alpha = jnp.exp2(m - m_new)
        nsub = bkv // sub
        cmax_next = cmax if last else scores(n + 1, 1 - bank, 0)
        pv = None
        for j in range(nsub):
            if j + 1 < nsub and not last:
                cmax_next = jnp.maximum(cmax_next, scores(n + 1, 1 - bank, j + 1))
            p = jnp.exp2(s_scr[bank][j * sub:(j + 1) * sub, :] - m_new).astype(jnp.bfloat16)
            koff = pl.multiple_of(chunk * bkv + j * sub, sub)
            part = _dot(vt_ref[0, head, :, pl.ds(koff, sub)], p)
            pv = part if pv is None else pv + part
        acc = alpha * acc_ref[...] + pv
        acc_ref[...] = acc

        @pl.when(chunk == nkv - 1)
        def _():
            denom = acc[V_HEAD_DIM:V_HEAD_DIM + 1]
            gate = gt_ref[0, rows(head, V_HEAD_DIM), pl.ds(qoff, bq)].astype(jnp.float32)
            out = acc[:V_HEAD_DIM] / denom * gate
            o_ref[0, rows(head, V_HEAD_DIM), pl.ds(qoff, bq)] = out.astype(o_ref.dtype)

        return m_new, cmax_next

    acc_ref[...] = jnp.zeros(acc_ref.shape, acc_ref.dtype)
    cmax0 = scores(0, 0, 0)
    for j in range(1, bkv // sub):
        cmax0 = jnp.maximum(cmax0, scores(0, 0, j))
    carry = (jnp.zeros((1, bq), jnp.float32), cmax0)

    assert tpb % 2 == 0 and ntrips % tpb == 0

    def body(i, carry, *, last=False):
        for t in range(tpb):
            carry = trip(tpb * i + t, t % 2, carry, last=last and t == tpb - 1)
        return carry

    carry = lax.fori_loop(0, ntrips // tpb - 1, body, carry)
    body(ntrips // tpb - 1, carry, last=True)


def _attn_call(qt, k, vt, gt, *, bq, bkv, sub, tpb, heads, single_buffer):
    b, _, s = qt.shape
    grid = (b, N_HEADS // heads)
    mode = dict(pipeline_mode=pl.Buffered(1)) if single_buffer else {}
    return pl.pallas_call(
        functools.partial(_attn_kernel, bq=bq, bkv=bkv, sub=sub, tpb=tpb),
        grid=grid,
        in_specs=[
            pl.BlockSpec((1, heads * HEAD_PAD, s), lambda bi, h: (bi, h, 0)),
            pl.BlockSpec((1, heads, s, HEAD_PAD), lambda bi, h: (bi, h, 0, 0)),
            pl.BlockSpec((1, heads, V_ROWS, s), lambda bi, h: (bi, h, 0, 0), **mode),
            pl.BlockSpec((1, heads * V_HEAD_DIM, s), lambda bi, h: (bi, h, 0), **mode),
        ],
        out_specs=pl.BlockSpec((1, heads * V_HEAD_DIM, s), lambda bi, h: (bi, h, 0)),
        out_shape=jax.ShapeDtypeStruct((b, ATTN_WIDTH, s), jnp.bfloat16),
        scratch_shapes=[pltpu.VMEM((bkv, bq), jnp.float32)] * 2
        + [pltpu.VMEM((V_ROWS, bq), jnp.float32)],
        compiler_params=pltpu.CompilerParams(
            dimension_semantics=("parallel", "parallel"),
            vmem_limit_bytes=VMEM_LIMIT_BYTES),
        name="attn",
    )(qt, k, vt, gt)


def _out_kernel(x_ref, cu_ref, cup_ref, cun_ref, g_ref, at_ref, convw_ref, wa_ref, wb_ref,
                npost_ref, y_ref):
    ti = pl.program_id(1)
    nt = pl.num_programs(1)
    cu = cu_ref[0].astype(jnp.float32)
    tm = cu.shape[0]
    prev_row = cup_ref[0, BF16_SUBLANES - 1:BF16_SUBLANES, :].astype(jnp.float32)
    next_row = cun_ref[0, 0:1, :].astype(jnp.float32)
    prev_row = jnp.where(ti > 0, prev_row, 0.0)
    next_row = jnp.where(ti < nt - 1, next_row, 0.0)
    row = lax.broadcasted_iota(jnp.int32, cu.shape, 0)
    up = jnp.where(row == 0, prev_row, pltpu.roll(cu, 1, 0))
    dn = jnp.where(row == tm - 1, next_row, pltpu.roll(cu, tm - 1, 0))
    w = convw_ref[...]
    conv = up * w[0:1] + cu * w[1:2] + dn * w[2:3]
    conv_out = (conv * g_ref[0].astype(jnp.float32)).astype(jnp.bfloat16)
    out = _dot(conv_out, wa_ref[...])
    out = out + lax.dot_general(at_ref[0], wb_ref[...], TN_DIMS,
                                preferred_element_type=jnp.float32)
    y_ref[0] = x_ref[0] + _rms(out, npost_ref[...])


def _out_call(x, cu, g, at, conv_w, wa, wb, npost, *, tm):
    b, s, d = x.shape
    grid = (b, s // tm)
    halo = BF16_SUBLANES
    nb = tm // halo
    last = s // halo - 1
    full = lambda arr: pl.BlockSpec(arr.shape, lambda bi, ti: (0,) * arr.ndim)
    return pl.pallas_call(
        _out_kernel,
        grid=grid,
        in_specs=[
            pl.BlockSpec((1, tm, d), lambda bi, ti: (bi, ti, 0)),
            pl.BlockSpec((1, tm, CONV_WIDTH), lambda bi, ti: (bi, ti, 0)),
            pl.BlockSpec((1, halo, CONV_WIDTH),
                         lambda bi, ti: (bi, jnp.maximum(ti * nb - 1, 0), 0)),
            pl.BlockSpec((1, halo, CONV_WIDTH),
                         lambda bi, ti: (bi, jnp.minimum((ti + 1) * nb, last), 0)),
            pl.BlockSpec((1, tm, CONV_WIDTH), lambda bi, ti: (bi, ti, 0)),
            pl.BlockSpec((1, ATTN_WIDTH, tm), lambda bi, ti: (bi, 0, ti)),
            full(conv_w), full(wa), full(wb), full(npost),
        ],
        out_specs=pl.BlockSpec((1, tm, d), lambda bi, ti: (bi, ti, 0)),
        out_shape=jax.ShapeDtypeStruct((b, s, d), x.dtype),
        compiler_params=pltpu.CompilerParams(
            dimension_semantics=("parallel", "parallel"),
            vmem_limit_bytes=VMEM_LIMIT_BYTES),
        name="outproj",
    )(x, cu, cu, cu, g, at, conv_w, wa, wb, npost)


def _prep_weights(norm_pre, w_in, q_norm, w_uq, kv_norm, w_ukv):
    bf16 = jnp.bfloat16
    wconv = w_in[:, :OFF_QLAT].astype(bf16)
    wzt = w_in[:, OFF_ZATTN:].T.astype(bf16)
    kpe = w_in[:, OFF_KPE:OFF_ZATTN]
    kpe_sw = jnp.concatenate([kpe[:, HALF_ROPE:], kpe[:, :HALF_ROPE]], axis=1)
    pad = jnp.zeros((D_MODEL, QK_ROPE_DIM), w_in.dtype)
    wlat = jnp.concatenate([w_in[:, OFF_QLAT:OFF_KPE], pad, kpe_sw, kpe, pad], axis=1).astype(bf16)
    wuq = w_uq.reshape(Q_LORA_RANK, N_HEADS, QK_DIM)
    wuq = jnp.pad(wuq, ((0, 0), (0, 0), (0, HEAD_PAD - QK_DIM)))
    wuqt = wuq.reshape(Q_LORA_RANK, N_HEADS * HEAD_PAD).T.astype(bf16)
    wukv = w_ukv.reshape(KV_LORA_RANK, N_HEADS, QK_NOPE_DIM + V_HEAD_DIM)
    wuk = wukv[:, :, :QK_NOPE_DIM].reshape(KV_LORA_RANK, N_HEADS * QK_NOPE_DIM).astype(bf16)
    wuvt = wukv[:, :, QK_NOPE_DIM:].reshape(KV_LORA_RANK, ATTN_WIDTH).T.astype(bf16)
    return (norm_pre.reshape(1, -1), wconv, wlat, wzt, q_norm.reshape(1, -1), wuqt,
            kv_norm.reshape(1, -1), wuk, wuvt)


def _rope_tables(seq_len):
    freqs = 1.0 / (ROPE_THETA ** (jnp.arange(0, QK_ROPE_DIM, 2, dtype=jnp.float32) / QK_ROPE_DIM))
    ang_t = freqs[:, None] * jnp.arange(seq_len, dtype=jnp.float32)[None, :]
    cos_t, sin_t = jnp.cos(ang_t), jnp.sin(ang_t)
    zl = jnp.zeros((QK_NOPE_DIM, seq_len), jnp.float32)
    zr = jnp.zeros((HEAD_PAD - QK_DIM, seq_len), jnp.float32)
    cos2 = jnp.concatenate([zl, cos_t, cos_t, zr], axis=0).T
    sin2 = jnp.concatenate([zl, -sin_t, sin_t, zr], axis=0).T
    return cos2, sin2, cos_t, sin_t


def _layer(x, weights, tables, conv_w, wa, wb, npost, *, tm_proj, bq, bkv, sub, tpb, tm_out):
    cu, g, qt, k, vt, gt = _proj_call(x, weights, tables, tm=tm_proj)
    bkv = min(bkv, x.shape[1])
    heads = max(1, min(N_HEADS, ATTN_WINDOW_ROWS // x.shape[1]))
    at = _attn_call(qt, k, vt, gt, bq=bq, bkv=bkv, sub=min(sub, bkv), tpb=tpb, heads=heads,
                    single_buffer=2 * bkv * bq * 4 > SCORE_BANKS_DOUBLE_BUFFER_LIMIT)
    return _out_call(x, cu, g, at, conv_w, wa, wb, npost, tm=tm_out)


def kernel(x_prompt, x_sample, norm_pre, w_in, conv_w, q_norm, w_uq, kv_norm, w_ukv, w_out, norm_post):
    y_prompt, y_sample = x_prompt, x_sample
    tables = _rope_tables(max(x_prompt.shape[1], x_sample.shape[1]))
    for l in range(norm_pre.shape[0]):
        weights = _prep_weights(norm_pre[l], w_in[l], q_norm[l], w_uq[l], kv_norm[l], w_ukv[l])
        wa = w_out[l, :CONV_WIDTH].astype(jnp.bfloat16)
        wb = w_out[l, CONV_WIDTH:].astype(jnp.bfloat16)
        npost = norm_post[l].reshape(1, -1)
        cfg = dict(tm_proj=512, bq=512, bkv=8192, sub=256, tm_out=1024)
        y_prompt = _layer(y_prompt, weights, tables, conv_w[l], wa, wb, npost, tpb=4, **cfg)
        y_sample = _layer(y_sample, weights, tables, conv_w[l], wa, wb, npost, tpb=2, **cfg)
    return (y_prompt, y_sample)
```

```python
import functools
import math

import jax
import jax.numpy as jnp
from jax import lax
from jax.experimental import pallas as pl
from jax.experimental.pallas import tpu as pltpu

D_MODEL = 1024
CONV_WIDTH = 512
N_HEADS = 8
QK_NOPE_DIM = 64
QK_ROPE_DIM = 32
HALF_ROPE = QK_ROPE_DIM // 2
V_HEAD_DIM = 64
ATTN_WIDTH = N_HEADS * V_HEAD_DIM
Q_LORA_RANK = 384
KV_LORA_RANK = 256
ROPE_THETA = 10000.0
NORM_EPS = 1e-6
QK_DIM = QK_NOPE_DIM + QK_ROPE_DIM
HEAD_PAD = 128
V_ROWS = 80
LAT_WIDTH = Q_LORA_RANK + KV_LORA_RANK
OFF_QLAT = 4 * CONV_WIDTH
OFF_KVLAT = OFF_QLAT + Q_LORA_RANK
OFF_KPE = OFF_KVLAT + KV_LORA_RANK
OFF_ZATTN = OFF_KPE + QK_ROPE_DIM

BF16_SUBLANES = 16
VMEM_LIMIT_BYTES = 60 * 1024 * 1024
SCORE_BANKS_DOUBLE_BUFFER_LIMIT = 16 * 1024 * 1024
ATTN_WINDOW_ROWS = 16384

NT_DIMS = (((1,), (1,)), ((), ()))
TN_DIMS = (((0,), (0,)), ((), ()))


def _rms(x, g):
    return x * lax.rsqrt(jnp.mean(x * x, axis=-1, keepdims=True) + NORM_EPS) * g


def _silu(x):
    return x * (1.0 / (1.0 + jnp.exp(-x)))


def _dot(a, b):
    return jnp.dot(a, b, preferred_element_type=jnp.float32)


def _proj_kernel(x_ref, npre_ref, wconv_ref, wlat_ref, wzt_ref, qnorm_ref,
                 wuqt_ref, kvnorm_ref, wuk_ref, wuvt_ref, cos2_ref, sin2_ref, cost_ref,
                 sint_ref, cu_ref, g_ref, qt_ref, k_ref, vt_ref, gt_ref, *, q_scale):
    bf16 = jnp.bfloat16
    x = x_ref[0]
    tm = x.shape[0]
    hdn = _rms(x, npre_ref[...]).astype(bf16)

    pc = _dot(hdn, wconv_ref[...])
    cw = CONV_WIDTH
    cu_ref[0] = (pc[:, 2 * cw:3 * cw] * pc[:, 0:cw]).astype(bf16)
    g_ref[0] = (pc[:, cw:2 * cw] * _silu(pc[:, 3 * cw:4 * cw])).astype(bf16)

    lat = _dot(hdn, wlat_ref[...])
    qn = _rms(lat[:, :Q_LORA_RANK], qnorm_ref[...]).astype(bf16)
    kvn = _rms(lat[:, Q_LORA_RANK:LAT_WIDTH], kvnorm_ref[...]).astype(bf16)

    qt = lax.dot_general(wuqt_ref[...], qn, NT_DIMS, preferred_element_type=jnp.float32)
    qt = qt * q_scale
    qt_ref[0] = qt.astype(bf16)
    cos_t = cost_ref[...]
    sin_t = sint_ref[...]
    for h in range(N_HEADS):
        r1 = h * HEAD_PAD + QK_NOPE_DIM
        r2 = r1 + HALF_ROPE
        x1 = qt[r1:r2]
        x2 = qt[r2:r2 + HALF_ROPE]
        qt_ref[0, r1:r2, :] = (x1 * cos_t - x2 * sin_t).astype(bf16)
        qt_ref[0, r2:r2 + HALF_ROPE, :] = (x2 * cos_t + x1 * sin_t).astype(bf16)

    kn = _dot(kvn, wuk_ref[...])
    kp = lat[:, LAT_WIDTH:]
    roped = kp * cos2_ref[...] + pltpu.roll(kp, QK_ROPE_DIM, 1) * sin2_ref[...]
    low_half = lax.broadcasted_iota(jnp.int32, (tm, HEAD_PAD), 1) < QK_NOPE_DIM
    for h in range(0, N_HEADS, 2):
        pair = kn[:, h * QK_NOPE_DIM:(h + 2) * QK_NOPE_DIM]
        k_ref[0, h] = jnp.where(low_half, pair, roped).astype(bf16)
        odd = pltpu.roll(pair, QK_NOPE_DIM, 1)
        k_ref[0, h + 1] = jnp.where(low_half, odd, roped).astype(bf16)

    vt = lax.dot_general(wuvt_ref[...], kvn, NT_DIMS, preferred_element_type=jnp.float32)
    row = lax.broadcasted_iota(jnp.int32, (V_ROWS - V_HEAD_DIM, tm), 0)
    ones_rows = jnp.where(row == 0, 1.0, 0.0).astype(bf16)
    for h in range(N_HEADS):
        vt_ref[0, h, 0:V_HEAD_DIM, :] = vt[h * V_HEAD_DIM:(h + 1) * V_HEAD_DIM].astype(bf16)
        vt_ref[0, h, V_HEAD_DIM:V_ROWS, :] = ones_rows

    zt = lax.dot_general(wzt_ref[...], hdn, NT_DIMS, preferred_element_type=jnp.float32)
    gt_ref[0] = _silu(zt).astype(bf16)


def _proj_call(x, weights, tables, *, tm):
    b, s, d = x.shape
    bf16 = jnp.bfloat16
    grid = (b, s // tm)
    full = lambda arr: pl.BlockSpec(arr.shape, lambda bi, ti: (0,) * arr.ndim,
                                    pipeline_mode=pl.Buffered(1))
    (npre, wconv, wlat, wzt, qnorm, wuqt, kvnorm, wuk, wuvt) = weights
    cos2, sin2, cos_t, sin_t = tables
    in_specs = [
        pl.BlockSpec((1, tm, d), lambda bi, ti: (bi, ti, 0)),
        full(npre), full(wconv), full(wlat), full(wzt), full(qnorm),
        full(wuqt), full(kvnorm), full(wuk), full(wuvt),
        pl.BlockSpec((tm, HEAD_PAD), lambda bi, ti: (ti, 0)),
        pl.BlockSpec((tm, HEAD_PAD), lambda bi, ti: (ti, 0)),
        pl.BlockSpec((HALF_ROPE, tm), lambda bi, ti: (0, ti)),
        pl.BlockSpec((HALF_ROPE, tm), lambda bi, ti: (0, ti)),
    ]
    out_shape = [
        jax.ShapeDtypeStruct((b, s, CONV_WIDTH), bf16),
        jax.ShapeDtypeStruct((b, s, CONV_WIDTH), bf16),
        jax.ShapeDtypeStruct((b, N_HEADS * HEAD_PAD, s), bf16),
        jax.ShapeDtypeStruct((b, N_HEADS, s, HEAD_PAD), bf16),
        jax.ShapeDtypeStruct((b, N_HEADS, V_ROWS, s), bf16),
        jax.ShapeDtypeStruct((b, ATTN_WIDTH, s), bf16),
    ]
    out_specs = [
        pl.BlockSpec((1, tm, CONV_WIDTH), lambda bi, ti: (bi, ti, 0)),
        pl.BlockSpec((1, tm, CONV_WIDTH), lambda bi, ti: (bi, ti, 0)),
        pl.BlockSpec((1, N_HEADS * HEAD_PAD, tm), lambda bi, ti: (bi, 0, ti)),
        pl.BlockSpec((1, N_HEADS, tm, HEAD_PAD), lambda bi, ti: (bi, 0, ti, 0)),
        pl.BlockSpec((1, N_HEADS, V_ROWS, tm), lambda bi, ti: (bi, 0, 0, ti)),
        pl.BlockSpec((1, ATTN_WIDTH, tm), lambda bi, ti: (bi, 0, ti)),
    ]
    q_scale = math.log2(math.e) / math.sqrt(QK_DIM)
    return pl.pallas_call(
        functools.partial(_proj_kernel, q_scale=q_scale),
        grid=grid, in_specs=in_specs, out_specs=out_specs, out_shape=out_shape,
        compiler_params=pltpu.CompilerParams(
            dimension_semantics=("parallel", "parallel"),
            vmem_limit_bytes=VMEM_LIMIT_BYTES),
        name="proj",
    )(x, npre, wconv, wlat, wzt, qnorm, wuqt, kvnorm, wuk, wuvt, cos2, sin2, cos_t, sin_t)


def _attn_kernel(qt_ref, k_ref, vt_ref, gt_ref, o_ref, s_a, s_b, acc_ref, *, bq, bkv, sub, tpb):
    s_scr = (s_a, s_b)
    heads = k_ref.shape[1]
    s_len = k_ref.shape[2]
    nkv = s_len // bkv
    nq = s_len // bq
    ntrips = heads * nq * nkv
    assert nkv & (nkv - 1) == 0 and nq & (nq - 1) == 0
    kv_shift = nkv.bit_length() - 1
    q_shift = nq.bit_length() - 1

    def split(n):
        head = (n >> (kv_shift + q_shift)) if heads > 1 else 0
        qoff = pl.multiple_of(((n >> kv_shift) & (nq - 1)) * bq, bq)
        return head, qoff, n & (nkv - 1)

    def rows(head, width):
        return pl.ds(pl.multiple_of(head * width, width), width) if heads > 1 else slice(None)

    def scores(n, bank, j):
        head, qoff, chunk = split(n)
        koff = pl.multiple_of(chunk * bkv + j * sub, sub)
        sc = _dot(k_ref[0, head, pl.ds(koff, sub), :],
                  qt_ref[0, rows(head, HEAD_PAD), pl.ds(qoff, bq)])
        s_scr[bank][j * sub:(j + 1) * sub, :] = sc
        return jnp.max(sc, axis=0, keepdims=True)

    def trip(n, bank, carry, *, last=False):
        m, cmax = carry
        head, qoff, chunk = split(n)
        m = jnp.where(chunk == 0, -jnp.inf, m)
        m_new = jnp.maximum(m, cmax)
        alpha = jnp.exp2(m - m_new)
        nsub = bkv // sub
        cmax_next = cmax if last else scores(n + 1, 1 - bank, 0)
        pv = None
        for j in range(nsub):
            if j + 1 < nsub and not last:
                cmax_next = jnp.maximum(cmax_next, scores(n + 1, 1 - bank, j + 1))
            p = jnp.exp2(s_scr[bank][j * sub:(j + 1) * sub, :] - m_new).astype(jnp.bfloat16)
            koff = pl.multiple_of(chunk * bkv + j * sub, sub)
            part = _dot(vt_ref[0, head, :, pl.ds(koff, sub)], p)
            pv = part if pv is None else pv + part
        acc = alpha * acc_ref[...] + pv
        acc_ref[...] = acc

        @pl.when(chunk == nkv - 1)
        def _():
            denom = acc[V_HEAD_DIM:V_HEAD_DIM + 1]
            gate = gt_ref[0, rows(head, V_HEAD_DIM), pl.ds(qoff, bq)].astype(jnp.float32)
            out = acc[:V_HEAD_DIM] / denom * gate
            o_ref[0, rows(head, V_HEAD_DIM), pl.ds(qoff, bq)] = out.astype(o_ref.dtype)

        return m_new, cmax_next

    acc_ref[...] = jnp.zeros(acc_ref.shape, acc_ref.dtype)
    cmax0 = scores(0, 0, 0)
    for j in range(1, bkv // sub):
        cmax0 = jnp.maximum(cmax0, scores(0, 0, j))
    carry = (jnp.zeros((1, bq), jnp.float32), cmax0)

    assert tpb % 2 == 0 and ntrips % tpb == 0

    def body(i, carry, *, last=False):
        for t in range(tpb):
            carry = trip(tpb * i + t, t % 2, carry, last=last and t == tpb - 1)
        return carry

    carry = lax.fori_loop(0, ntrips // tpb - 1, body, carry)
    body(ntrips // tpb - 1, carry, last=True)


def _attn_call(qt, k, vt, gt, *, bq, bkv, sub, tpb, heads, single_buffer):
    b, _, s = qt.shape
    grid = (b, N_HEADS // heads)
    mode = dict(pipeline_mode=pl.Buffered(1)) if single_buffer else {}
    return pl.pallas_call(
        functools.partial(_attn_kernel, bq=bq, bkv=bkv, sub=sub, tpb=tpb),
        grid=grid,
        in_specs=[
            pl.BlockSpec((1, heads * HEAD_PAD, s), lambda bi, h: (bi, h, 0)),
            pl.BlockSpec((1, heads, s, HEAD_PAD), lambda bi, h: (bi, h, 0, 0)),
            pl.BlockSpec((1, heads, V_ROWS, s), lambda bi, h: (bi, h, 0, 0), **mode),
            pl.BlockSpec((1, heads * V_HEAD_DIM, s), lambda bi, h: (bi, h, 0), **mode),
        ],
        out_specs=pl.BlockSpec((1, heads * V_HEAD_DIM, s), lambda bi, h: (bi, h, 0)),
        out_shape=jax.ShapeDtypeStruct((b, ATTN_WIDTH, s), jnp.bfloat16),
        scratch_shapes=[pltpu.VMEM((bkv, bq), jnp.float32)] * 2
        + [pltpu.VMEM((V_ROWS, bq), jnp.float32)],
        compiler_params=pltpu.CompilerParams(
            dimension_semantics=("parallel", "parallel"),
            vmem_limit_bytes=VMEM_LIMIT_BYTES),
        name="attn",
    )(qt, k, vt, gt)


def _out_kernel(x_ref, cu_ref, cup_ref, cun_ref, g_ref, at_ref, convw_ref, wa_ref, wb_ref,
                npost_ref, y_ref):
    ti = pl.program_id(1)
    nt = pl.num_programs(1)
    cu = cu_ref[0].astype(jnp.float32)
    tm = cu.shape[0]
    prev_row = cup_ref[0, BF16_SUBLANES - 1:BF16_SUBLANES, :].astype(jnp.float32)
    next_row = cun_ref[0, 0:1, :].astype(jnp.float32)
    prev_row = jnp.where(ti > 0, prev_row, 0.0)
    next_row = jnp.where(ti < nt - 1, next_row, 0.0)
    row = lax.broadcasted_iota(jnp.int32, cu.shape, 0)
    up = jnp.where(row == 0, prev_row, pltpu.roll(cu, 1, 0))
    dn = jnp.where(row == tm - 1, next_row, pltpu.roll(cu, tm - 1, 0))
    w = convw_ref[...]
    conv = up * w[0:1] + cu * w[1:2] + dn * w[2:3]
    conv_out = (conv * g_ref[0].astype(jnp.float32)).astype(jnp.bfloat16)
    out = _dot(conv_out, wa_ref[...])
    out = out + lax.dot_general(at_ref[0], wb_ref[...], TN_DIMS,
                                preferred_element_type=jnp.float32)
    y_ref[0] = x_ref[0] + _rms(out, npost_ref[...])


def _out_call(x, cu, g, at, conv_w, wa, wb, npost, *, tm):
    b, s, d = x.shape
    grid = (b, s // tm)
    halo = BF16_SUBLANES
    nb = tm // halo
    last = s // halo - 1
    full = lambda arr: pl.BlockSpec(arr.shape, lambda bi, ti: (0,) * arr.ndim)
    return pl.pallas_call(
        _out_kernel,
        grid=grid,
        in_specs=[
            pl.BlockSpec((1, tm, d), lambda bi, ti: (bi, ti, 0)),
            pl.BlockSpec((1, tm, CONV_WIDTH), lambda bi, ti: (bi, ti, 0)),
            pl.BlockSpec((1, halo, CONV_WIDTH),
                         lambda bi, ti: (bi, jnp.maximum(ti * nb - 1, 0), 0)),
            pl.BlockSpec((1, halo, CONV_WIDTH),
                         lambda bi, ti: (bi, jnp.minimum((ti + 1) * nb, last), 0)),
            pl.BlockSpec((1, tm, CONV_WIDTH), lambda bi, ti: (bi, ti, 0)),
            pl.BlockSpec((1, ATTN_WIDTH, tm), lambda bi, ti: (bi, 0, ti)),
            full(conv_w), full(wa), full(wb), full(npost),
        ],
        out_specs=pl.BlockSpec((1, tm, d), lambda bi, ti: (bi, ti, 0)),
        out_shape=jax.ShapeDtypeStruct((b, s, d), x.dtype),
        compiler_params=pltpu.CompilerParams(
            dimension_semantics=("parallel", "parallel"),
            vmem_limit_bytes=VMEM_LIMIT_BYTES),
        name="outproj",
    )(x, cu, cu, cu, g, at, conv_w, wa, wb, npost)


def _prep_weights(norm_pre, w_in, q_norm, w_uq, kv_norm, w_ukv):
    bf16 = jnp.bfloat16
    wconv = w_in[:, :OFF_QLAT].astype(bf16)
    wzt = w_in[:, OFF_ZATTN:].T.astype(bf16)
    kpe = w_in[:, OFF_KPE:OFF_ZATTN]
    kpe_sw = jnp.concatenate([kpe[:, HALF_ROPE:], kpe[:, :HALF_ROPE]], axis=1)
    pad = jnp.zeros((D_MODEL, QK_ROPE_DIM), w_in.dtype)
    wlat = jnp.concatenate([w_in[:, OFF_QLAT:OFF_KPE], pad, kpe_sw, kpe, pad], axis=1).astype(bf16)
    wuq = w_uq.reshape(Q_LORA_RANK, N_HEADS, QK_DIM)
    wuq = jnp.pad(wuq, ((0, 0), (0, 0), (0, HEAD_PAD - QK_DIM)))
    wuqt = wuq.reshape(Q_LORA_RANK, N_HEADS * HEAD_PAD).T.astype(bf16)
    wukv = w_ukv.reshape(KV_LORA_RANK, N_HEADS, QK_NOPE_DIM + V_HEAD_DIM)
    wuk = wukv[:, :, :QK_NOPE_DIM].reshape(KV_LORA_RANK, N_HEADS * QK_NOPE_DIM).astype(bf16)
    wuvt = wukv[:, :, QK_NOPE_DIM:].reshape(KV_LORA_RANK, ATTN_WIDTH).T.astype(bf16)
    return (norm_pre.reshape(1, -1), wconv, wlat, wzt, q_norm.reshape(1, -1), wuqt,
            kv_norm.reshape(1, -1), wuk, wuvt)


def _rope_tables(seq_len):
    freqs = 1.0 / (ROPE_THETA ** (jnp.arange(0, QK_ROPE_DIM, 2, dtype=jnp.float32) / QK_ROPE_DIM))
    ang_t = freqs[:, None] * jnp.arange(seq_len, dtype=jnp.float32)[None, :]
    cos_t, sin_t = jnp.cos(ang_t), jnp.sin(ang_t)
    zl = jnp.zeros((QK_NOPE_DIM, seq_len), jnp.float32)
    zr = jnp.zeros((HEAD_PAD - QK_DIM, seq_len), jnp.float32)
    cos2 = jnp.concatenate([zl, cos_t, cos_t, zr], axis=0).T
    sin2 = jnp.concatenate([zl, -sin_t, sin_t, zr], axis=0).T
    return cos2, sin2, cos_t, sin_t


def _layer(x, weights, tables, conv_w, wa, wb, npost, *, tm_proj, bq, bkv, sub, tpb, tm_out):
    cu, g, qt, k, vt, gt = _proj_call(x, weights, tables, tm=tm_proj)
    bkv = min(bkv, x.shape[1])
    heads = max(1, min(N_HEADS, ATTN_WINDOW_ROWS // x.shape[1]))
    at = _attn_call(qt, k, vt, gt, bq=bq, bkv=bkv, sub=min(sub, bkv), tpb=tpb, heads=heads,
                    single_buffer=2 * bkv * bq * 4 > SCORE_BANKS_DOUBLE_BUFFER_LIMIT)
    return _out_call(x, cu, g, at, conv_w, wa, wb, npost, tm=tm_out)


def kernel(x_prompt, x_sample, norm_pre, w_in, conv_w, q_norm, w_uq, kv_norm, w_ukv, w_out, norm_post):
    y_prompt, y_sample = x_prompt, x_sample
    tables = _rope_tables(max(x_prompt.shape[1], x_sample.shape[1]))
    for l in range(norm_pre.shape[0]):
        weights = _prep_weights(norm_pre[l], w_in[l], q_norm[l], w_uq[l], kv_norm[l], w_ukv[l])
        wa = w_out[l, :CONV_WIDTH].astype(jnp.bfloat16)
        wb = w_out[l, CONV_WIDTH:].astype(jnp.bfloat16)
        npost = norm_post[l].reshape(1, -1)
        cfg = dict(tm_proj=1024, bq=512, bkv=8192, sub=256, tm_out=1024)
        y_prompt = _layer(y_prompt, weights, tables, conv_w[l], wa, wb, npost, tpb=8, **cfg)
        y_sample = _layer(y_sample, weights, tables, conv_w[l], wa, wb, npost, tpb=2, **cfg)
    return (y_prompt, y_sample)
```

```python
import functools
import math

import jax
import jax.numpy as jnp
from jax import lax
from jax.experimental import pallas as pl
from jax.experimental.pallas import tpu as pltpu

D_MODEL = 1024
CONV_WIDTH = 512
N_HEADS = 8
QK_NOPE_DIM = 64
QK_ROPE_DIM = 32
HALF_ROPE = QK_ROPE_DIM // 2
V_HEAD_DIM = 64
ATTN_WIDTH = N_HEADS * V_HEAD_DIM
Q_LORA_RANK = 384
KV_LORA_RANK = 256
ROPE_THETA = 10000.0
NORM_EPS = 1e-6
QK_DIM = QK_NOPE_DIM + QK_ROPE_DIM
HEAD_PAD = 128
V_ROWS = 80
LAT_WIDTH = Q_LORA_RANK + KV_LORA_RANK
OFF_QLAT = 4 * CONV_WIDTH
OFF_KVLAT = OFF_QLAT + Q_LORA_RANK
OFF_KPE = OFF_KVLAT + KV_LORA_RANK
OFF_ZATTN = OFF_KPE + QK_ROPE_DIM

BF16_SUBLANES = 16
VMEM_LIMIT_BYTES = 60 * 1024 * 1024
SCORE_BANKS_DOUBLE_BUFFER_LIMIT = 16 * 1024 * 1024
ATTN_WINDOW_ROWS = 16384

NT_DIMS = (((1,), (1,)), ((), ()))
TN_DIMS = (((0,), (0,)), ((), ()))


def _rms(x, g):
    return x * lax.rsqrt(jnp.mean(x * x, axis=-1, keepdims=True) + NORM_EPS) * g


def _silu(x):
    return x * (1.0 / (1.0 + jnp.exp(-x)))


def _dot(a, b):
    return jnp.dot(a, b, preferred_element_type=jnp.float32)


def _proj_kernel(x_ref, npre_ref, wconv_ref, wlat_ref, wzt_ref, qnorm_ref,
                 wuqt_ref, kvnorm_ref, wuk_ref, wuvt_ref, cos2_ref, sin2_ref, cost_ref,
                 sint_ref, cu_ref, g_ref, qt_ref, k_ref, vt_ref, gt_ref, *, q_scale):
    bf16 = jnp.bfloat16
    x = x_ref[0]
    tm = x.shape[0]
    hdn = _rms(x, npre_ref[...]).astype(bf16)

    pc = _dot(hdn, wconv_ref[...])
    cw = CONV_WIDTH
    cu_ref[0] = (pc[:, 2 * cw:3 * cw] * pc[:, 0:cw]).astype(bf16)
    g_ref[0] = (pc[:, cw:2 * cw] * _silu(pc[:, 3 * cw:4 * cw])).astype(bf16)

    lat = _dot(hdn, wlat_ref[...])
    qn = _rms(lat[:, :Q_LORA_RANK], qnorm_ref[...]).astype(bf16)
    kvn = _rms(lat[:, Q_LORA_RANK:LAT_WIDTH], kvnorm_ref[...]).astype(bf16)

    qt = lax.dot_general(wuqt_ref[...], qn, NT_DIMS, preferred_element_type=jnp.float32)
    qt = qt * q_scale
    qt_ref[0] = qt.astype(bf16)
    cos_t = cost_ref[...]
    sin_t = sint_ref[...]
    for h in range(N_HEADS):
        r1 = h * HEAD_PAD + QK_NOPE_DIM
        r2 = r1 + HALF_ROPE
        x1 = qt[r1:r2]
        x2 = qt[r2:r2 + HALF_ROPE]
        qt_ref[0, r1:r2, :] = (x1 * cos_t - x2 * sin_t).astype(bf16)
        qt_ref[0, r2:r2 + HALF_ROPE, :] = (x2 * cos_t + x1 * sin_t).astype(bf16)

    kn = _dot(kvn, wuk_ref[...])
    kp = lat[:, LAT_WIDTH:]
    roped = kp * cos2_ref[...] + pltpu.roll(kp, QK_ROPE_DIM, 1) * sin2_ref[...]
    low_half = lax.broadcasted_iota(jnp.int32, (tm, HEAD_PAD), 1) < QK_NOPE_DIM
    for h in range(0, N_HEADS, 2):
        pair = kn[:, h * QK_NOPE_DIM:(h + 2) * QK_NOPE_DIM]
        k_ref[0, h] = jnp.where(low_half, pair, roped).astype(bf16)
        odd = pltpu.roll(pair, QK_NOPE_DIM, 1)
        k_ref[0, h + 1] = jnp.where(low_half, odd, roped).astype(bf16)

    vt = lax.dot_general(wuvt_ref[...], kvn, NT_DIMS, preferred_element_type=jnp.float32)
    row = lax.broadcasted_iota(jnp.int32, (V_ROWS - V_HEAD_DIM, tm), 0)
    ones_rows = jnp.where(row == 0, 1.0, 0.0).astype(bf16)
    for h in range(N_HEADS):
        vt_ref[0, h, 0:V_HEAD_DIM, :] = vt[h * V_HEAD_DIM:(h + 1) * V_HEAD_DIM].astype(bf16)
        vt_ref[0, h, V_HEAD_DIM:V_ROWS, :] = ones_rows

    zt = lax.dot_general(wzt_ref[...], hdn, NT_DIMS, preferred_element_type=jnp.float32)
    gt_ref[0] = _silu(zt).astype(bf16)


def _proj_call(x, weights, tables, *, tm):
    b, s, d = x.shape
    bf16 = jnp.bfloat16
    grid = (b, s // tm)
    full = lambda arr: pl.BlockSpec(arr.shape, lambda bi, ti: (0,) * arr.ndim,
                                    pipeline_mode=pl.Buffered(1))
    (npre, wconv, wlat, wzt, qnorm, wuqt, kvnorm, wuk, wuvt) = weights
    cos2, sin2, cos_t, sin_t = tables
    in_specs = [
        pl.BlockSpec((1, tm, d), lambda bi, ti: (bi, ti, 0)),
        full(npre), full(wconv), full(wlat), full(wzt), full(qnorm),
        full(wuqt), full(kvnorm), full(wuk), full(wuvt),
        pl.BlockSpec((tm, HEAD_PAD), lambda bi, ti: (ti, 0)),
        pl.BlockSpec((tm, HEAD_PAD), lambda bi, ti: (ti, 0)),
        pl.BlockSpec((HALF_ROPE, tm), lambda bi, ti: (0, ti)),
        pl.BlockSpec((HALF_ROPE, tm), lambda bi, ti: (0, ti)),
    ]
    out_shape = [
        jax.ShapeDtypeStruct((b, s, CONV_WIDTH), bf16),
        jax.ShapeDtypeStruct((b, s, CONV_WIDTH), bf16),
        jax.ShapeDtypeStruct((b, N_HEADS * HEAD_PAD, s), bf16),
        jax.ShapeDtypeStruct((b, N_HEADS, s, HEAD_PAD), bf16),
        jax.ShapeDtypeStruct((b, N_HEADS, V_ROWS, s), bf16),
        jax.ShapeDtypeStruct((b, ATTN_WIDTH, s), bf16),
    ]
    out_specs = [
        pl.BlockSpec((1, tm, CONV_WIDTH), lambda bi, ti: (bi, ti, 0)),
        pl.BlockSpec((1, tm, CONV_WIDTH), lambda bi, ti: (bi, ti, 0)),
        pl.BlockSpec((1, N_HEADS * HEAD_PAD, tm), lambda bi, ti: (bi, 0, ti)),
        pl.BlockSpec((1, N_HEADS, tm, HEAD_PAD), lambda bi, ti: (bi, 0, ti, 0)),
        pl.BlockSpec((1, N_HEADS, V_ROWS, tm), lambda bi, ti: (bi, 0, 0, ti)),
        pl.BlockSpec((1, ATTN_WIDTH, tm), lambda bi, ti: (bi, 0, ti)),
    ]
    q_scale = math.log2(math.e) / math.sqrt(QK_DIM)
    return pl.pallas_call(
        functools.partial(_proj_kernel, q_scale=q_scale),
        grid=grid, in_specs=in_specs, out_specs=out_specs, out_shape=out_shape,
        compiler_params=pltpu.CompilerParams(
            dimension_semantics=("parallel", "parallel"),
            vmem_limit_bytes=VMEM_LIMIT_BYTES),
        name="proj",
    )(x, npre, wconv, wlat, wzt, qnorm, wuqt, kvnorm, wuk, wuvt, cos2, sin2, cos_t, sin_t)


def _attn_kernel(qt_ref, k_ref, vt_ref, gt_ref, o_ref, s_a, s_b, acc_ref, *, bq, bkv, sub, tpb):
    s_scr = (s_a, s_b)
    heads = k_ref.shape[1]
    s_len = k_ref.shape[2]
    nkv = s_len // bkv
    nq = s_len // bq
    ntrips = heads * nq * nkv
    assert nkv & (nkv - 1) == 0 and nq & (nq - 1) == 0
    kv_shift = nkv.bit_length() - 1
    q_shift = nq.bit_length() - 1

    def split(n):
        head = (n >> (kv_shift + q_shift)) if heads > 1 else 0
        qoff = pl.multiple_of(((n >> kv_shift) & (nq - 1)) * bq, bq)
        return head, qoff, n & (nkv - 1)

    def rows(head, width):
        return pl.ds(pl.multiple_of(head * width, width), width) if heads > 1 else slice(None)

    def scores(n, bank, j):
        head, qoff, chunk = split(n)
        koff = pl.multiple_of(chunk * bkv + j * sub, sub)
        sc = _dot(k_ref[0, head, pl.ds(koff, sub), :],
                  qt_ref[0, rows(head, HEAD_PAD), pl.ds(qoff, bq)])
        s_scr[bank][j * sub:(j + 1) * sub, :] = sc
        return jnp.max(sc, axis=0, keepdims=True)

    def trip(n, bank, carry, *, last=False):
        m, cmax = carry
        head, qoff, chunk = split(n)
        m = jnp.where(chunk == 0, -jnp.inf, m)
        m_new = jnp.maximum(m, cmax)
        alpha = jnp.exp2(m - m_new)
        nsub = bkv // sub
        cmax_next = cmax if last else scores(n + 1, 1 - bank, 0)
        pv = None
        for j in range(nsub):
            if j + 1 < nsub and not last:
                cmax_next = jnp.maximum(cmax_next, scores(n + 1, 1 - bank, j + 1))
            p = jnp.exp2((s_scr[bank][j * sub:(j + 1) * sub, :] - m_new).astype(jnp.bfloat16))
            koff = pl.multiple_of(chunk * bkv + j * sub, sub)
            part = _dot(vt_ref[0, head, :, pl.ds(koff, sub)], p)
            pv = part if pv is None else pv + part
        acc = alpha * acc_ref[...] + pv
        acc_ref[...] = acc

        @pl.when(chunk == nkv - 1)
        def _():
            denom = acc[V_HEAD_DIM:V_HEAD_DIM + 1]
            gate = gt_ref[0, rows(head, V_HEAD_DIM), pl.ds(qoff, bq)].astype(jnp.float32)
            out = acc[:V_HEAD_DIM] / denom * gate
            o_ref[0, rows(head, V_HEAD_DIM), pl.ds(qoff, bq)] = out.astype(o_ref.dtype)

        return m_new, cmax_next

    acc_ref[...] = jnp.zeros(acc_ref.shape, acc_ref.dtype)
    cmax0 = scores(0, 0, 0)
    for j in range(1, bkv // sub):
        cmax0 = jnp.maximum(cmax0, scores(0, 0, j))
    carry = (jnp.zeros((1, bq), jnp.float32), cmax0)

    assert tpb % 2 == 0 and ntrips % tpb == 0

    def body(i, carry, *, last=False):
        for t in range(tpb):
            carry = trip(tpb * i + t, t % 2, carry, last=last and t == tpb - 1)
        return carry

    carry = lax.fori_loop(0, ntrips // tpb - 1, body, carry)
    body(ntrips // tpb - 1, carry, last=True)


def _attn_call(qt, k, vt, gt, *, bq, bkv, sub, tpb, heads, single_buffer):
    b, _, s = qt.shape
    grid = (b, N_HEADS // heads)
    mode = dict(pipeline_mode=pl.Buffered(1)) if single_buffer else {}
    return pl.pallas_call(
        functools.partial(_attn_kernel, bq=bq, bkv=bkv, sub=sub, tpb=tpb),
        grid=grid,
        in_specs=[
            pl.BlockSpec((1, heads * HEAD_PAD, s), lambda bi, h: (bi, h, 0), **mode),
            pl.BlockSpec((1, heads, s, HEAD_PAD), lambda bi, h: (bi, h, 0, 0)),
            pl.BlockSpec((1, heads, V_ROWS, s), lambda bi, h: (bi, h, 0, 0), **mode),
            pl.BlockSpec((1, heads * V_HEAD_DIM, s), lambda bi, h: (bi, h, 0), **mode),
        ],
        out_specs=pl.BlockSpec((1, heads * V_HEAD_DIM, s), lambda bi, h: (bi, h, 0)),
        out_shape=jax.ShapeDtypeStruct((b, ATTN_WIDTH, s), jnp.bfloat16),
        scratch_shapes=[pltpu.VMEM((bkv, bq), jnp.float32)] * 2
        + [pltpu.VMEM((V_ROWS, bq), jnp.float32)],
        compiler_params=pltpu.CompilerParams(
            dimension_semantics=("parallel", "parallel"),
            vmem_limit_bytes=VMEM_LIMIT_BYTES),
        name="attn",
    )(qt, k, vt, gt)


def _out_kernel(x_ref, cu_ref, cup_ref, cun_ref, g_ref, at_ref, convw_ref, wa_ref, wb_ref,
                npost_ref, y_ref):
    ti = pl.program_id(1)
    nt = pl.num_programs(1)
    cu = cu_ref[0].astype(jnp.float32)
    tm = cu.shape[0]
    prev_row = cup_ref[0, BF16_SUBLANES - 1:BF16_SUBLANES, :].astype(jnp.float32)
    next_row = cun_ref[0, 0:1, :].astype(jnp.float32)
    prev_row = jnp.where(ti > 0, prev_row, 0.0)
    next_row = jnp.where(ti < nt - 1, next_row, 0.0)
    row = lax.broadcasted_iota(jnp.int32, cu.shape, 0)
    up = jnp.where(row == 0, prev_row, pltpu.roll(cu, 1, 0))
    dn = jnp.where(row == tm - 1, next_row, pltpu.roll(cu, tm - 1, 0))
    w = convw_ref[...]
    conv = up * w[0:1] + cu * w[1:2] + dn * w[2:3]
    conv_out = (conv * g_ref[0].astype(jnp.float32)).astype(jnp.bfloat16)
    out = _dot(conv_out, wa_ref[...])
    out = out + lax.dot_general(at_ref[0], wb_ref[...], TN_DIMS,
                                preferred_element_type=jnp.float32)
    y_ref[0] = x_ref[0] + _rms(out, npost_ref[...])


def _out_call(x, cu, g, at, conv_w, wa, wb, npost, *, tm):
    b, s, d = x.shape
    grid = (b, s // tm)
    halo = BF16_SUBLANES
    nb = tm // halo
    last = s // halo - 1
    full = lambda arr: pl.BlockSpec(arr.shape, lambda bi, ti: (0,) * arr.ndim)
    return pl.pallas_call(
        _out_kernel,
        grid=grid,
        in_specs=[
            pl.BlockSpec((1, tm, d), lambda bi, ti: (bi, ti, 0)),
            pl.BlockSpec((1, tm, CONV_WIDTH), lambda bi, ti: (bi, ti, 0)),
            pl.BlockSpec((1, halo, CONV_WIDTH),
                         lambda bi, ti: (bi, jnp.maximum(ti * nb - 1, 0), 0)),
            pl.BlockSpec((1, halo, CONV_WIDTH),
                         lambda bi, ti: (bi, jnp.minimum((ti + 1) * nb, last), 0)),
            pl.BlockSpec((1, tm, CONV_WIDTH), lambda bi, ti: (bi, ti, 0)),
            pl.BlockSpec((1, ATTN_WIDTH, tm), lambda bi, ti: (bi, 0, ti)),
            full(conv_w), full(wa), full(wb), full(npost),
        ],
        out_specs=pl.BlockSpec((1, tm, d), lambda bi, ti: (bi, ti, 0)),
        out_shape=jax.ShapeDtypeStruct((b, s, d), x.dtype),
        compiler_params=pltpu.CompilerParams(
            dimension_semantics=("parallel", "parallel"),
            vmem_limit_bytes=VMEM_LIMIT_BYTES),
        name="outproj",
    )(x, cu, cu, cu, g, at, conv_w, wa, wb, npost)


def _prep_weights(norm_pre, w_in, q_norm, w_uq, kv_norm, w_ukv):
    bf16 = jnp.bfloat16
    wconv = w_in[:, :OFF_QLAT].astype(bf16)
    wzt = w_in[:, OFF_ZATTN:].T.astype(bf16)
    kpe = w_in[:, OFF_KPE:OFF_ZATTN]
    kpe_sw = jnp.concatenate([kpe[:, HALF_ROPE:], kpe[:, :HALF_ROPE]], axis=1)
    pad = jnp.zeros((D_MODEL, QK_ROPE_DIM), w_in.dtype)
    wlat = jnp.concatenate([w_in[:, OFF_QLAT:OFF_KPE], pad, kpe_sw, kpe, pad], axis=1).astype(bf16)
    wuq = w_uq.reshape(Q_LORA_RANK, N_HEADS, QK_DIM)
    wuq = jnp.pad(wuq, ((0, 0), (0, 0), (0, HEAD_PAD - QK_DIM)))
    wuqt = wuq.reshape(Q_LORA_RANK, N_HEADS * HEAD_PAD).T.astype(bf16)
    wukv = w_ukv.reshape(KV_LORA_RANK, N_HEADS, QK_NOPE_DIM + V_HEAD_DIM)
    wuk = wukv[:, :, :QK_NOPE_DIM].reshape(KV_LORA_RANK, N_HEADS * QK_NOPE_DIM).astype(bf16)
    wuvt = wukv[:, :, QK_NOPE_DIM:].reshape(KV_LORA_RANK, ATTN_WIDTH).T.astype(bf16)
    return (norm_pre.reshape(1, -1), wconv, wlat, wzt, q_norm.reshape(1, -1), wuqt,
            kv_norm.reshape(1, -1), wuk, wuvt)


def _rope_tables(seq_len):
    freqs = 1.0 / (ROPE_THETA ** (jnp.arange(0, QK_ROPE_DIM, 2, dtype=jnp.float32) / QK_ROPE_DIM))
    ang_t = freqs[:, None] * jnp.arange(seq_len, dtype=jnp.float32)[None, :]
    cos_t, sin_t = jnp.cos(ang_t), jnp.sin(ang_t)
    zl = jnp.zeros((QK_NOPE_DIM, seq_len), jnp.float32)
    zr = jnp.zeros((HEAD_PAD - QK_DIM, seq_len), jnp.float32)
    cos2 = jnp.concatenate([zl, cos_t, cos_t, zr], axis=0).T
    sin2 = jnp.concatenate([zl, -sin_t, sin_t, zr], axis=0).T
    return cos2, sin2, cos_t, sin_t


def _layer(x, weights, tables, conv_w, wa, wb, npost, *, tm_proj, bq, bkv, sub, tpb, tm_out):
    cu, g, qt, k, vt, gt = _proj_call(x, weights, tables, tm=tm_proj)
    bkv = min(bkv, x.shape[1])
    heads = max(1, min(N_HEADS, ATTN_WINDOW_ROWS // x.shape[1]))
    at = _attn_call(qt, k, vt, gt, bq=bq, bkv=bkv, sub=min(sub, bkv), tpb=tpb, heads=heads,
                    single_buffer=2 * bkv * bq * 4 > SCORE_BANKS_DOUBLE_BUFFER_LIMIT)
    return _out_call(x, cu, g, at, conv_w, wa, wb, npost, tm=tm_out)


def kernel(x_prompt, x_sample, norm_pre, w_in, conv_w, q_norm, w_uq, kv_norm, w_ukv, w_out, norm_post):
    y_prompt, y_sample = x_prompt, x_sample
    tables = _rope_tables(max(x_prompt.shape[1], x_sample.shape[1]))
    for l in range(norm_pre.shape[0]):
        weights = _prep_weights(norm_pre[l], w_in[l], q_norm[l], w_uq[l], kv_norm[l], w_ukv[l])
        wa = w_out[l, :CONV_WIDTH].astype(jnp.bfloat16)
        wb = w_out[l, CONV_WIDTH:].astype(jnp.bfloat16)
        npost = norm_post[l].reshape(1, -1)
        cfg = dict(tm_proj=1024, bq=512, bkv=8192, sub=256, tm_out=1024)
        y_prompt = _layer(y_prompt, weights, tables, conv_w[l], wa, wb, npost, tpb=8, **cfg)
        y_sample = _layer(y_sample, weights, tables, conv_w[l], wa, wb, npost, tpb=2, **cfg)
    return (y_prompt, y_sample)
```
